```python
import jax, jax.numpy as jnp
from jax import lax
import numpy as np

D_MODEL = 1024
BATCH = 2
SEQ = 8192
DEPTH = 2

GRID_W = 64
CTX_LEN = 256
HEAD_DIM = 64
BLOCK = 128
WINDOW = 128
A_HEADS = 4
A_KV_HEADS = 2
B_HEADS = 4
B_KV_HEADS = 2
C_HEADS = 8
C_Q_RANK = 256
C_KV_RANK = 128
C_NOPE_DIM = 64
C_ROPE_DIM = 32
C_V_DIM = 64
D_MIX = (A_HEADS + B_HEADS) * HEAD_DIM + C_HEADS * C_V_DIM
D_FF = 4 * D_MODEL
ROPE_THETA = 10000.0
NORM_EPS = 1e-6
NEG_INF = -1e30
DEEPNORM_ALPHA = (2 * DEPTH) ** 0.25
DEEPNORM_BETA = (8 * DEPTH) ** -0.25
IN_SPLITS = (A_HEADS * HEAD_DIM, A_KV_HEADS * HEAD_DIM, A_KV_HEADS * HEAD_DIM,
             B_HEADS * HEAD_DIM, B_KV_HEADS * HEAD_DIM, B_KV_HEADS * HEAD_DIM,
             C_Q_RANK, C_KV_RANK, C_ROPE_DIM)
IN_OFFSETS = tuple(int(o) for o in np.cumsum(IN_SPLITS)[:-1])
D_IN = int(sum(IN_SPLITS))

kernel_name = "hymba_style_hybrid_dit_block"

F32 = jnp.float32


def _rms_norm(x, g):
    xf = x.astype(F32)
    y = xf * lax.rsqrt(jnp.mean(xf * xf, axis=-1, keepdims=True) + NORM_EPS)
    return (y * g.astype(F32)).astype(x.dtype)


def _layer_norm(x, g, b):
    xf = x.astype(F32)
    mu = jnp.mean(xf, axis=-1, keepdims=True)
    var = jnp.mean(jnp.square(xf - mu), axis=-1, keepdims=True)
    y = (xf - mu) * lax.rsqrt(var + NORM_EPS) * g.astype(F32) + b.astype(F32)
    return y.astype(x.dtype)


def _axial_rope_tables(n_rows, rot_dim):
    row = jnp.repeat(jnp.arange(n_rows, dtype=F32), GRID_W)
    col = jnp.tile(jnp.arange(GRID_W, dtype=F32), n_rows)
    n_freq = rot_dim // 4
    inv = ROPE_THETA ** (-jnp.arange(n_freq, dtype=F32) / n_freq)
    ang = jnp.concatenate([row[:, None] * inv, col[:, None] * inv], axis=-1)
    return jnp.cos(ang), jnp.sin(ang)


def _apply_rope(x, cos, sin):
    half = x.shape[-1] // 2
    xf = x.astype(F32)
    x1, x2 = xf[..., :half], xf[..., half:]
    c, s = cos[None, :, None, :], sin[None, :, None, :]
    return jnp.concatenate([x1 * c - x2 * s, x1 * s + x2 * c], axis=-1).astype(x.dtype)


def _heads(t, n_heads):
    return t.reshape(t.shape[0], t.shape[1], n_heads, t.shape[-1] // n_heads)


def _merge(t):
    return t.reshape(t.shape[0], t.shape[1], -1)


def _gqa_core(q, k, v, sink=None):
    bsz, n_q, n_h, dh = q.shape
    n_kv = k.shape[2]
    grp = n_h // n_kv
    n_keys = k.shape[1]
    qg = q.reshape(bsz, n_q, n_kv, grp, dh)
    s = jnp.einsum('bqkgd,bskd->bkgqs', qg, k).astype(F32) * (dh ** -0.5)
    if sink is not None:
        sink_logit = jnp.broadcast_to(sink.astype(F32).reshape(n_kv, grp)[None, :, :, None, None], s.shape[:-1] + (1,))
        s = jnp.concatenate([s, sink_logit], axis=-1)
    p = jax.nn.softmax(s, axis=-1)[..., :n_keys].astype(v.dtype)
    out = jnp.einsum('bkgqs,bskd->bqkgd', p, v)
    return out.reshape(bsz, n_q, n_h, dh)


def _mla_core(q_nope, q_rope, k_nope, k_rope, v):
    scale = (C_NOPE_DIM + C_ROPE_DIM) ** -0.5
    s = jnp.einsum('bqhd,bshd->bhqs', q_nope, k_nope) + jnp.einsum('bqhr,bsr->bhqs', q_rope, k_rope)
    p = jax.nn.softmax(s.astype(F32) * scale, axis=-1).astype(v.dtype)
    return jnp.einsum('bhqs,bshd->bqhd', p, v)


def _sweep_query_blocks(fn, qs):
    nb = qs[0].shape[1] // BLOCK
    to_blocks = lambda t: jnp.moveaxis(t.reshape(t.shape[0], nb, BLOCK, *t.shape[2:]), 1, 0)
    out = lax.map(lambda qb: fn(*qb), tuple(to_blocks(t) for t in qs))
    out = jnp.moveaxis(out, 0, 1)
    return out.reshape(out.shape[0], nb * BLOCK, *out.shape[3:])


def _window_attn(q, k, v, k_ctx, v_ctx, sink):
    bsz, n_tok, n_h, dh = q.shape
    n_kv = k.shape[2]
    grp = n_h // n_kv
    nb = n_tok // BLOCK
    n_loc, n_ctx = 3 * BLOCK, k_ctx.shape[1]
    scale = dh ** -0.5
    qb = q.reshape(bsz, nb, BLOCK, n_kv, grp, dh)
    pad = ((0, 0), (BLOCK, BLOCK), (0, 0), (0, 0))
    band = (jnp.arange(nb) * BLOCK)[:, None] + jnp.arange(n_loc)[None, :]
    kb = jnp.pad(k, pad)[:, band]
    vb = jnp.pad(v, pad)[:, band]
    s_loc = jnp.einsum('bnqkgd,bnskd->bnkgqs', qb, kb).astype(F32) * scale
    s_ctx = jnp.einsum('bnqkgd,bckd->bnkgqc', qb, k_ctx).astype(F32) * scale
    q_pos = (jnp.arange(nb) * BLOCK)[:, None] + jnp.arange(BLOCK)[None, :]
    k_pos = band - BLOCK
    rel = k_pos[:, None, :] - q_pos[:, :, None]
    valid = (jnp.abs(rel) <= WINDOW) & (k_pos[:, None, :] >= 0) & (k_pos[:, None, :] < n_tok)
    s_loc = jnp.where(valid[None, :, None, None], s_loc, NEG_INF)
    sink_logit = jnp.broadcast_to(sink.astype(F32).reshape(n_kv, grp)[None, None, :, :, None, None], s_loc.shape[:-1] + (1,))
    p = jax.nn.softmax(jnp.concatenate([s_loc, s_ctx, sink_logit], axis=-1), axis=-1).astype(v.dtype)
    out = (jnp.einsum('bnkgqs,bnskd->bnqkgd', p[..., :n_loc], vb)
           + jnp.einsum('bnkgqc,bckd->bnqkgd', p[..., n_loc:n_loc + n_ctx], v_ctx))
    return out.reshape(bsz, n_tok, n_h, dh)


def _token_mixers(h, hc, w_in, sink_a, q_norm_b, k_norm_b, mla_q_norm, mla_kv_norm,
                  w_uq, w_uk, w_uv, w_out, rope64, rope32, with_ctx_out):
    cos64, sin64 = rope64
    cos32, sin32 = rope32
    a_q, a_k, a_v, b_q, b_k, b_v, c_q, c_kv, c_kr = jnp.split(h @ w_in, IN_OFFSETS, axis=-1)
    a_qc, a_kc, a_vc, b_qc, b_kc, b_vc, c_qc, c_kvc, c_krc = jnp.split(hc @ w_in, IN_OFFSETS, axis=-1)

    q_a = _apply_rope(_heads(a_q, A_HEADS), cos64, sin64)
    k_a = _apply_rope(_heads(a_k, A_KV_HEADS), cos64, sin64)
    v_a = _heads(a_v, A_KV_HEADS)
    k_a_ctx, v_a_ctx = _heads(a_kc, A_KV_HEADS), _heads(a_vc, A_KV_HEADS)
    y_a = _window_attn(q_a, k_a, v_a, k_a_ctx, v_a_ctx, sink_a)

    q_b = _apply_rope(_rms_norm(_heads(b_q, B_HEADS), q_norm_b), cos64, sin64)
    k_b = _apply_rope(_rms_norm(_heads(b_k, B_KV_HEADS), k_norm_b), cos64, sin64)
    v_b = _heads(b_v, B_KV_HEADS)
    k_b_ctx = _rms_norm(_heads(b_kc, B_KV_HEADS), k_norm_b)
    v_b_ctx = _heads(b_vc, B_KV_HEADS)
    k_b_all = jnp.concatenate([k_b_ctx, k_b], axis=1)
    v_b_all = jnp.concatenate([v_b_ctx, v_b], axis=1)
    y_b = _sweep_query_blocks(lambda qi: _gqa_core(qi, k_b_all, v_b_all), (q_b,))

    def mla_q(cq):
        q = _heads(_rms_norm(cq, mla_q_norm) @ w_uq, C_HEADS)
        return q[..., :C_NOPE_DIM], q[..., C_NOPE_DIM:]

    def mla_kv(ckv):
        ckv = _rms_norm(ckv, mla_kv_norm)
        return _heads(ckv @ w_uk, C_HEADS), _heads(ckv @ w_uv, C_HEADS)

    q_c_nope, q_c_rope = mla_q(c_q)
    q_c_rope = _apply_rope(q_c_rope, cos32, sin32)
    k_c_nope, v_c = mla_kv(c_kv)
    k_c_rope = _apply_rope(c_kr[:, :, None, :], cos32, sin32)[:, :, 0, :]
    k_c_nope_ctx, v_c_ctx = mla_kv(c_kvc)
    k_c_nope_all = jnp.concatenate([k_c_nope_ctx, k_c_nope], axis=1)
    k_c_rope_all = jnp.concatenate([c_krc, k_c_rope], axis=1)
    v_c_all = jnp.concatenate([v_c_ctx, v_c], axis=1)
    y_c = _sweep_query_blocks(lambda qn, qr: _mla_core(qn, qr, k_c_nope_all, k_c_rope_all, v_c_all),
                              (q_c_nope, q_c_rope))

    y = jnp.concatenate([_merge(y_a), _merge(y_b), _merge(y_c)], axis=-1) @ w_out
    if not with_ctx_out:
        return y, None

    yc_a = _gqa_core(_heads(a_qc, A_HEADS), k_a_ctx, v_a_ctx, sink_a)
    yc_b = _gqa_core(_rms_norm(_heads(b_qc, B_HEADS), q_norm_b), k_b_ctx, v_b_ctx)
    qc_nope, qc_rope = mla_q(c_qc)
    yc_c = _mla_core(qc_nope, qc_rope, k_c_nope_ctx, c_krc, v_c_ctx)
    yc = jnp.concatenate([_merge(yc_a), _merge(yc_b), _merge(yc_c)], axis=-1) @ w_out
    return y, yc


def _sq_relu_mlp(h, w1, w2):
    a = jnp.maximum(h @ w1, 0)
    return (a * a) @ w2


def setup_inputs(seed: int = 0) -> dict:
    key = jax.random.key(seed)
    ks = iter(jax.random.split(key, 32))
    nrm = lambda shape, scale: jax.random.normal(next(ks), shape, F32) * scale
    gain = lambda shape: 1.0 + nrm(shape, 0.02)
    L = DEPTH
    return {
        "x": nrm((BATCH, SEQ, D_MODEL), 1.0),
        "c": nrm((BATCH, D_MODEL), 1.0),
        "ctx": nrm((BATCH, CTX_LEN, D_MODEL), 1.0),
        "c_ctx": nrm((D_MODEL,), 1.0),
        "w_mod": nrm((L, D_MODEL, 6 * D_MODEL), 0.5 * D_MODEL ** -0.5),
        "b_mod": nrm((L, 6 * D_MODEL), 0.01),
        "w_in": nrm((L, D_MODEL, D_IN), D_MODEL ** -0.5),
        "sink_a": nrm((L, A_HEADS), 0.5),
        "q_norm_b": gain((L, HEAD_DIM)),
        "k_norm_b": gain((L, HEAD_DIM)),
        "mla_q_norm": gain((L, C_Q_RANK)),
        "mla_kv_norm": gain((L, C_KV_RANK)),
        "w_uq": nrm((L, C_Q_RANK, C_HEADS * (C_NOPE_DIM + C_ROPE_DIM)), C_Q_RANK ** -0.5),
        "w_uk": nrm((L, C_KV_RANK, C_HEADS * C_NOPE_DIM), C_KV_RANK ** -0.5),
        "w_uv": nrm((L, C_KV_RANK, C_HEADS * C_V_DIM), C_KV_RANK ** -0.5),
        "w_out": nrm((L, D_MIX, D_MODEL), DEEPNORM_BETA * D_MIX ** -0.5),
        "ln1_g": gain((L, D_MODEL)),
        "ln1_b": nrm((L, D_MODEL), 0.02),
        "w_fc1": nrm((L, D_MODEL, D_FF), D_MODEL ** -0.5),
        "w_fc2": nrm((L, D_FF, D_MODEL), DEEPNORM_BETA * D_FF ** -0.5),
        "ln2_g": gain((L, D_MODEL)),
        "ln2_b": nrm((L, D_MODEL), 0.02),
    }


def reference(x, c, ctx, c_ctx, w_mod, b_mod, w_in, sink_a, q_norm_b, k_norm_b, mla_q_norm, mla_kv_norm,
              w_uq, w_uk, w_uv, w_out, ln1_g, ln1_b, w_fc1, w_fc2, ln2_g, ln2_b):
    n_rows = x.shape[1] // GRID_W
    rope64 = _axial_rope_tables(n_rows, HEAD_DIM)
    rope32 = _axial_rope_tables(n_rows, C_ROPE_DIM)
    xc = ctx
    for l in range(DEPTH):
        last = l == DEPTH - 1
        mod = jax.nn.silu(c) @ w_mod[l] + b_mod[l]
        mod_c = jax.nn.silu(c_ctx) @ w_mod[l] + b_mod[l]
        sh1, sc1, g1, sh2, sc2, g2 = jnp.split(mod[:, None, :], 6, axis=-1)
        sh1c, sc1c, g1c, sh2c, sc2c, g2c = jnp.split(mod_c, 6, axis=-1)

        h = x * (1 + sc1) + sh1
        hc = xc * (1 + sc1c) + sh1c
        y, yc = _token_mixers(h, hc, w_in[l], sink_a[l], q_norm_b[l], k_norm_b[l], mla_q_norm[l], mla_kv_norm[l],
                              w_uq[l], w_uk[l], w_uv[l], w_out[l], rope64, rope32, not last)
        x = _layer_norm(DEEPNORM_ALPHA * x + g1 * y, ln1_g[l], ln1_b[l])

        h = x * (1 + sc2) + sh2
        x = _layer_norm(DEEPNORM_ALPHA * x + g2 * _sq_relu_mlp(h, w_fc1[l], w_fc2[l]), ln2_g[l], ln2_b[l])

        if not last:
            xc = _layer_norm(DEEPNORM_ALPHA * xc + g1c * yc, ln1_g[l], ln1_b[l])
            hc = xc * (1 + sc2c) + sh2c
            xc = _layer_norm(DEEPNORM_ALPHA * xc + g2c * _sq_relu_mlp(hc, w_fc1[l], w_fc2[l]), ln2_g[l], ln2_b[l])
    return x
```

```python
import functools

import jax
import jax.numpy as jnp
from jax import lax
from jax.experimental import pallas as pl
from jax.experimental.pallas import tpu as pltpu

F32 = jnp.float32
BF16 = jnp.bfloat16

D_MODEL = 1024
DEPTH = 2
GRID_W = 64
HEAD_DIM = 64
WINDOW = 128
A_HEADS, A_KV_HEADS = 4, 2
B_HEADS, B_KV_HEADS = 4, 2
C_HEADS = 8
C_Q_RANK, C_KV_RANK = 256, 128
C_NOPE_DIM, C_ROPE_DIM, C_V_DIM = 64, 32, 64
D_FF = 4 * D_MODEL
ROPE_THETA = 10000.0
NORM_EPS = 1e-6
NEG_INF = -1e30
DEEPNORM_ALPHA = (2 * DEPTH) ** 0.25

OFF_AQ, OFF_AK, OFF_AV = 0, 256, 384
OFF_BQ, OFF_BK, OFF_BV = 512, 768, 896
OFF_CQ, OFF_CKV, OFF_CKR = 1024, 1280, 1408
D_IN = 1440
D_IN_PAD = 1536
LANES = 128
C_QK_PAD = 128

VMEM_LIMIT = 56 * 1024 * 1024


def _cparams(sem):
    return pltpu.CompilerParams(dimension_semantics=sem, vmem_limit_bytes=VMEM_LIMIT)


def _full(shape):
    n = len(shape)
    return pl.BlockSpec(shape, lambda *_: (0,) * n)


def _mod_kernel(cc_ref, w_ref, b_ref, o_ref):
    cc = cc_ref[...]
    s = cc / (1.0 + jnp.exp(-cc))
    o_ref[0] = jnp.dot(s, w_ref[0], preferred_element_type=F32,
                       precision=lax.Precision.HIGHEST) + b_ref[0]


def _modulation(cc, w_mod, b_mod):
    n_l = w_mod.shape[0]
    tn = 1024
    return pl.pallas_call(
        _mod_kernel,
        grid=(n_l, 6 * D_MODEL // tn),
        in_specs=[pl.BlockSpec((8, D_MODEL), lambda l, j: (0, 0)),
                  pl.BlockSpec((1, D_MODEL, tn), lambda l, j: (l, 0, j)),
                  pl.BlockSpec((1, 1, tn), lambda l, j: (l, 0, j))],
        out_specs=pl.BlockSpec((1, 8, tn), lambda l, j: (l, 0, j)),
        out_shape=jax.ShapeDtypeStruct((n_l, 8, 6 * D_MODEL), F32),
        compiler_params=_cparams(("parallel", "parallel")),
        name="modulation",
    )(cc, w_mod, b_mod.reshape(n_l, 1, 6 * D_MODEL))


def _rope(x, cos, sin_signed, half):
    width = x.shape[-1]
    lane = lax.broadcasted_iota(jnp.int32, x.shape, 1)
    first = (lane % (2 * half)) < half
    nxt = pltpu.roll(x, width - half, axis=1)
    prv = pltpu.roll(x, half, axis=1)
    return x * cos + jnp.where(first, nxt, prv) * sin_signed


def _group_mean_sq(x, g_ref):
    x2 = x * x
    hi = x2.astype(BF16)
    lo = (x2 - hi.astype(F32)).astype(BF16)
    g = g_ref[...]
    return (jnp.dot(hi, g, preferred_element_type=F32) + jnp.dot(lo, g, preferred_element_type=F32))


def _row_rms(x, gain):
    ms = jnp.mean(x * x, axis=-1, keepdims=True)
    return x * lax.rsqrt(ms + NORM_EPS) * gain


def _inproj_kernel(x_ref, mod_ref, w_in_ref, w_uq_ref, w_uk_ref, w_uv_ref,
                   gq_ref, gk_ref, gmat_ref, mqn_ref, mkvn_ref,
                   cos_a_ref, sin_a_ref, cos_q_ref, sin_q_ref, cos_k_ref, sin_k_ref,
                   qa_ref, ka_ref, va_ref, qb_ref, kb_ref, vb_ref, qc_ref, kc_ref, vc_ref,
                   *, rope):
    x = x_ref[0]
    shift, scale = mod_ref[0, 0:1, :], mod_ref[0, 1:2, :]
    h = (x * (1.0 + scale) + shift).astype(BF16)
    p = jnp.dot(h, w_in_ref[...], preferred_element_type=F32)

    def store_heads(ref, val, n_heads, width):
        for hh in range(n_heads):
            ref[0, hh] = val[:, hh * width:(hh + 1) * width].astype(ref.dtype)

    qa = p[:, OFF_AQ:OFF_AQ + 256]
    ka = p[:, OFF_AK:OFF_AK + 128]
    if rope:
        qa = _rope(qa, cos_a_ref[...], sin_a_ref[...], HEAD_DIM // 2)
        ka = _rope(ka, cos_a_ref[:, 0:128], sin_a_ref[:, 0:128], HEAD_DIM // 2)
    store_heads(qa_ref, qa * (HEAD_DIM ** -0.5), A_HEADS, HEAD_DIM)
    store_heads(ka_ref, ka, A_KV_HEADS, HEAD_DIM)
    store_heads(va_ref, p[:, OFF_AV:OFF_AV + 128], A_KV_HEADS, HEAD_DIM)

    qb = p[:, OFF_BQ:OFF_BQ + 256]
    kb = p[:, OFF_BK:OFF_BK + 128]
    qb = qb * lax.rsqrt(_group_mean_sq(qb, gmat_ref) + NORM_EPS) * gq_ref[...]
    kb = kb * lax.rsqrt(_group_mean_sq(kb, gmat_ref.at[0:128, 0:128]) + NORM_EPS) * gk_ref[...]
    if rope:
        qb = _rope(qb, cos_a_ref[...], sin_a_ref[...], HEAD_DIM // 2)
        kb = _rope(kb, cos_a_ref[:, 0:128], sin_a_ref[:, 0:128], HEAD_DIM // 2)
    store_heads(qb_ref, qb * (HEAD_DIM ** -0.5), B_HEADS, HEAD_DIM)
    store_heads(kb_ref, kb, B_KV_HEADS, HEAD_DIM)
    store_heads(vb_ref, p[:, OFF_BV:OFF_BV + 128], B_KV_HEADS, HEAD_DIM)

    cq = _row_rms(p[:, OFF_CQ:OFF_CQ + C_Q_RANK], mqn_ref[...]).astype(BF16)
    ckv = _row_rms(p[:, OFF_CKV:OFF_CKV + C_KV_RANK], mkvn_ref[...]).astype(BF16)
    q = jnp.dot(cq, w_uq_ref[...], preferred_element_type=F32)
    kn = jnp.dot(ckv, w_uk_ref[...], preferred_element_type=F32)
    vc = jnp.dot(ckv, w_uv_ref[...], preferred_element_type=F32)
    kr = p[:, OFF_CKR:OFF_CKR + LANES]
    if rope:
        kr = _rope(kr, cos_k_ref[...], sin_k_ref[...], C_ROPE_DIM // 2)
    kr = pltpu.roll(kr, C_NOPE_DIM, axis=1)
    c_scale = (C_NOPE_DIM + C_ROPE_DIM) ** -0.5
    for hh in range(C_HEADS):
        qh = q[:, hh * C_QK_PAD:(hh + 1) * C_QK_PAD]
        if rope:
            qh = _rope(qh, cos_q_ref[...], sin_q_ref[...], C_ROPE_DIM // 2)
        qc_ref[0, hh] = (qh * c_scale).astype(BF16)
        kc_ref[0, hh] = (kn[:, hh * C_QK_PAD:(hh + 1) * C_QK_PAD] + kr).astype(BF16)
    store_heads(vc_ref, vc, C_HEADS, C_V_DIM)


def _inproj(x, mod, wl, tabs, *, rope, tm):
    bsz, n_tok, _ = x.shape
    nt = n_tok // tm
    tok = lambda width: pl.BlockSpec((tm, width), lambda i, b: (i, 0))
    heads = lambda n, width: pl.BlockSpec((1, n, tm, width), lambda i, b: (b, 0, i, 0))
    hshape = lambda n, width: jax.ShapeDtypeStruct((bsz, n, n_tok, width), BF16)
    return pl.pallas_call(
        functools.partial(_inproj_kernel, rope=rope),
        grid=(nt, bsz),
        in_specs=[pl.BlockSpec((1, tm, D_MODEL), lambda i, b: (b, i, 0)),
                  pl.BlockSpec((1, 6, D_MODEL), lambda i, b: (b, 0, 0)),
                  _full((D_MODEL, D_IN_PAD)), _full((C_Q_RANK, C_HEADS * C_QK_PAD)),
                  _full((C_KV_RANK, C_HEADS * C_QK_PAD)), _full((C_KV_RANK, C_HEADS * C_V_DIM)),
                  _full((1, 256)), _full((1, 128)), _full((256, 256)),
                  _full((1, C_Q_RANK)), _full((1, C_KV_RANK)),
                  tok(256), tok(256), tok(128), tok(128), tok(128), tok(128)],
        out_specs=[heads(A_HEADS, HEAD_DIM), heads(A_KV_HEADS, HEAD_DIM), heads(A_KV_HEADS, HEAD_DIM),
                   heads(B_HEADS, HEAD_DIM), heads(B_KV_HEADS, HEAD_DIM), heads(B_KV_HEADS, HEAD_DIM),
                   heads(C_HEADS, C_QK_PAD), heads(C_HEADS, C_QK_PAD), heads(C_HEADS, C_V_DIM)],
        out_shape=[hshape(A_HEADS, HEAD_DIM), hshape(A_KV_HEADS, HEAD_DIM), hshape(A_KV_HEADS, HEAD_DIM),
                   hshape(B_HEADS, HEAD_DIM), hshape(B_KV_HEADS, HEAD_DIM), hshape(B_KV_HEADS, HEAD_DIM),
                   hshape(C_HEADS, C_QK_PAD), hshape(C_HEADS, C_QK_PAD), hshape(C_HEADS, C_V_DIM)],
        compiler_params=_cparams(("parallel", "parallel")),
        name="inproj_rope" if rope else "inproj_ctx",
    )(x, mod, wl["w_in"], wl["w_uq"], wl["w_uk"], wl["w_uv"], wl["gq"], wl["gk"], wl["gmat"],
      wl["mqn"], wl["mkvn"], *tabs)


def _qk(q, k):
    return lax.dot_general(q, k, (((1,), (1,)), ((), ())), preferred_element_type=F32)


def _online_step(q, k, v, m, l, acc):
    s = _qk(q, k)
    m_new = jnp.maximum(m, jnp.max(s, axis=-1, keepdims=True))
    alpha = jnp.exp(m - m_new)
    p = jnp.exp(s - m_new)
    l = alpha * l + jnp.sum(p, axis=-1, keepdims=True)
    acc = alpha * acc + jnp.dot(p.astype(v.dtype), v, preferred_element_type=F32)
    return m_new, l, acc


def _softmax_attend(q, k, v, sink=None):
    s = _qk(q, k)
    m = jnp.max(s, axis=-1, keepdims=True)
    if sink is not None:
        m = jnp.maximum(m, sink)
    p = jnp.exp(s - m)
    l = jnp.sum(p, axis=-1, keepdims=True)
    if sink is not None:
        l = l + jnp.exp(sink - m)
    return jnp.dot(p.astype(v.dtype), v, preferred_element_type=F32) / l


def _flash_kernel(q_ref, kc_ref, vc_ref, kl_ref, vl_ref, o_ref, *, shared_kv, tk):
    tq = q_ref.shape[2]
    n_chunks = kl_ref.shape[2] // tk

    def attend(q, hk):
        rows = q.shape[0]
        m0 = jnp.full((rows, 1), NEG_INF, F32)
        l0 = jnp.zeros((rows, 1), F32)
        a0 = jnp.zeros((rows, vl_ref.shape[3]), F32)
        carry = _online_step(q, kc_ref[0, hk], vc_ref[0, hk], m0, l0, a0)

        def body(j, c):
            off = pl.multiple_of(j * tk, tk)
            return _online_step(q, kl_ref[0, hk, pl.ds(off, tk), :], vl_ref[0, hk, pl.ds(off, tk), :], *c)

        _, l, acc = lax.fori_loop(0, n_chunks, body, carry)
        return acc / l

    if shared_kv:
        o = attend(q_ref[0].reshape(2 * tq, q_ref.shape[3]), 0)
        o0, o1 = o[:tq], o[tq:]
    else:
        o0, o1 = attend(q_ref[0, 0], 0), attend(q_ref[0, 1], 1)
    o_ref[0] = jnp.concatenate([o0, o1], axis=-1).astype(o_ref.dtype)


def _flash(q, k_ctx, v_ctx, k_lat, v_lat, *, shared_kv, tq, tk):
    bsz, n_heads, n_tok, dk = q.shape
    n_ctx, dv = k_ctx.shape[2], v_lat.shape[3]
    kvb = 1 if shared_kv else 2
    kv_spec = lambda n, width: pl.BlockSpec((1, kvb, n, width), lambda b, hp, i: (b, hp, 0, 0))
    return pl.pallas_call(
        functools.partial(_flash_kernel, shared_kv=shared_kv, tk=tk),
        grid=(bsz, n_heads // 2, n_tok // tq),
        in_specs=[pl.BlockSpec((1, 2, tq, dk), lambda b, hp, i: (b, hp, i, 0)),
                  kv_spec(n_ctx, dk), kv_spec(n_ctx, dv), kv_spec(n_tok, dk), kv_spec(n_tok, dv)],
        out_specs=pl.BlockSpec((1, tq, 2 * dv), lambda b, hp, i: (b, i, hp)),
        out_shape=jax.ShapeDtypeStruct((bsz, n_tok, n_heads * dv), BF16),
        compiler_params=_cparams(("parallel", "parallel", "arbitrary")),
        name="flash_gqa" if shared_kv else "flash_mla",
    )(q, k_ctx, v_ctx, k_lat, v_lat)


def _window_kernel(sink_ref, q_ref, kc_ref, vc_ref, kl_ref, vl_ref, o_ref):
    tq = q_ref.shape[2]
    n_tok = kl_ref.shape[2]
    span = tq + 2 * WINDOW
    kvh, i = pl.program_id(1), pl.program_id(2)
    q0 = i * tq
    start = pl.multiple_of(jnp.clip(q0 - WINDOW, 0, n_tok - span), WINDOW)
    q = q_ref[0].reshape(2 * tq, q_ref.shape[3])
    k_loc = kl_ref[0, 0, pl.ds(start, span), :]
    v_loc = vl_ref[0, 0, pl.ds(start, span), :]

    s_loc = _qk(q, k_loc)
    row = lax.broadcasted_iota(jnp.int32, s_loc.shape, 0)
    col = lax.broadcasted_iota(jnp.int32, s_loc.shape, 1)
    q_pos = q0 + jnp.where(row >= tq, row - tq, row)
    rel = (start + col) - q_pos
    s_loc = jnp.where(jnp.abs(rel) <= WINDOW, s_loc, NEG_INF)
    s_ctx = _qk(q, kc_ref[0, 0])

    row1 = lax.broadcasted_iota(jnp.int32, (2 * tq, 1), 0)
    sink = jnp.where(row1 < tq, sink_ref[2 * kvh], sink_ref[2 * kvh + 1])
    m = jnp.maximum(jnp.maximum(jnp.max(s_loc, axis=-1, keepdims=True),
                                jnp.max(s_ctx, axis=-1, keepdims=True)), sink)
    p_loc = jnp.exp(s_loc - m)
    p_ctx = jnp.exp(s_ctx - m)
    l = (jnp.sum(p_loc, axis=-1, keepdims=True) + jnp.sum(p_ctx, axis=-1, keepdims=True)
         + jnp.exp(sink - m))
    o = (jnp.dot(p_loc.astype(BF16), v_loc, preferred_element_type=F32)
         + jnp.dot(p_ctx.astype(BF16), vc_ref[0, 0], preferred_element_type=F32)) / l
    o_ref[0] = jnp.concatenate([o[:tq], o[tq:]], axis=-1).astype(o_ref.dtype)


def _window_attn(sink, q, k_ctx, v_ctx, k_lat, v_lat, *, tq):
    bsz, n_heads, n_tok, d = q.shape
    n_ctx = k_ctx.shape[2]
    kv_spec = lambda n: pl.BlockSpec((1, 1, n, d), lambda b, kvh, i: (b, kvh, 0, 0))
    return pl.pallas_call(
        _window_kernel,
        grid=(bsz, n_heads // 2, n_tok // tq),
        in_specs=[pl.BlockSpec(memory_space=pltpu.SMEM),
                  pl.BlockSpec((1, 2, tq, d), lambda b, kvh, i: (b, kvh, i, 0)),
                  kv_spec(n_ctx), kv_spec(n_ctx), kv_spec(n_tok), kv_spec(n_tok)],
        out_specs=pl.BlockSpec((1, tq, 2 * d), lambda b, kvh, i: (b, i, kvh)),
        out_shape=jax.ShapeDtypeStruct((bsz, n_tok, n_heads * d), BF16),
        compiler_params=_cparams(("parallel", "parallel", "arbitrary")),
        name="window_gqa",
    )(sink, q, k_ctx, v_ctx, k_lat, v_lat)


def _ctx_attn_kernel(sink_ref, qa_ref, ka_ref, va_ref, qb_ref, kb_ref, vb_ref, qc_ref, kc_ref, vc_ref,
                     ya_ref, yb_ref, yc_ref):
    n = qa_ref.shape[2]
    row1 = lax.broadcasted_iota(jnp.int32, (2 * n, 1), 0)

    def gqa(q_ref, k_ref, v_ref, y_ref, with_sink):
        outs = []
        for kvh in range(k_ref.shape[1]):
            q = q_ref[0, 2 * kvh:2 * kvh + 2].reshape(2 * n, q_ref.shape[3])
            sink = jnp.where(row1 < n, sink_ref[2 * kvh], sink_ref[2 * kvh + 1]) if with_sink else None
            o = _softmax_attend(q, k_ref[0, kvh], v_ref[0, kvh], sink)
            outs += [o[:n], o[n:]]
        y_ref[0] = jnp.concatenate(outs, axis=-1).astype(y_ref.dtype)

    gqa(qa_ref, ka_ref, va_ref, ya_ref, True)
    gqa(qb_ref, kb_ref, vb_ref, yb_ref, False)
    outs = [_softmax_attend(qc_ref[0, hh], kc_ref[0, hh], vc_ref[0, hh]) for hh in range(C_HEADS)]
    yc_ref[0] = jnp.concatenate(outs, axis=-1).astype(yc_ref.dtype)


def _ctx_attn(sink, cx):
    bsz, _, n, _ = cx[0].shape
    spec = lambda a: pl.BlockSpec((1,) + a.shape[1:], lambda b: (b, 0, 0, 0))
    widths = (A_HEADS * HEAD_DIM, B_HEADS * HEAD_DIM, C_HEADS * C_V_DIM)
    return pl.pallas_call(
        _ctx_attn_kernel,
        grid=(bsz,),
        in_specs=[pl.BlockSpec(memory_space=pltpu.SMEM)] + [spec(a) for a in cx],
        out_specs=[pl.BlockSpec((1, n, w), lambda b: (b, 0, 0)) for w in widths],
        out_shape=[jax.ShapeDtypeStruct((bsz, n, w), BF16) for w in widths],
        compiler_params=_cparams(("parallel",)),
        name="ctx_attn",
    )(sink, *cx)


def _layer_norm(v, g, b):
    mu = jnp.mean(v, axis=-1, keepdims=True)
    d = v - mu
    var = jnp.mean(d * d, axis=-1, keepdims=True)
    return d * lax.rsqrt(var + NORM_EPS) * g + b


def _mlp_kernel(x_ref, ya_ref, yb_ref, yc_ref, mod_ref, w_out_ref, ln1_ref, w1_ref, w2_ref, ln2_ref,
                o_ref, *, ff_chunk):
    x = x_ref[0]
    g1, sh2, sc2, g2 = (mod_ref[0, r:r + 1, :] for r in (2, 3, 4, 5))
    y = (jnp.dot(ya_ref[0], w_out_ref[0:256, :], preferred_element_type=F32)
         + jnp.dot(yb_ref[0], w_out_ref[256:512, :], preferred_element_type=F32)
         + jnp.dot(yc_ref[0], w_out_ref[512:1024, :], preferred_element_type=F32))
    x1 = _layer_norm(DEEPNORM_ALPHA * x + g1 * y, ln1_ref[0:1, :], ln1_ref[1:2, :])
    h = (x1 * (1.0 + sc2) + sh2).astype(BF16)
    acc = jnp.zeros(x.shape, F32)
    for c0 in range(0, D_FF, ff_chunk):
        a = jnp.maximum(jnp.dot(h, w1_ref[:, c0:c0 + ff_chunk], preferred_element_type=F32), 0.0)
        acc = acc + jnp.dot((a * a).astype(BF16), w2_ref[c0:c0 + ff_chunk, :], preferred_element_type=F32)
    o_ref[0] = _layer_norm(DEEPNORM_ALPHA * x1 + g2 * acc, ln2_ref[0:1, :], ln2_ref[1:2, :])


def _mlp(x, ya, yb, yc, mod, wl, *, tm):
    bsz, n_tok, _ = x.shape
    tok = lambda width: pl.BlockSpec((1, tm, width), lambda b, i: (b, i, 0))
    const = lambda shape: pl.BlockSpec(shape, lambda b, i: (0, 0), pipeline_mode=pl.Buffered(1))
    return pl.pallas_call(
        functools.partial(_mlp_kernel, ff_chunk=1024),
        grid=(bsz, n_tok // tm),
        in_specs=[tok(D_MODEL), tok(256), tok(256), tok(512),
                  pl.BlockSpec((1, 6, D_MODEL), lambda b, i: (b, 0, 0)),
                  const((D_MODEL, D_MODEL)), const((2, D_MODEL)),
                  const((D_MODEL, D_FF)), const((D_FF, D_MODEL)), const((2, D_MODEL))],
        out_specs=tok(D_MODEL),
        out_shape=jax.ShapeDtypeStruct(x.shape, F32),
        compiler_params=_cparams(("parallel", "parallel")),
        name="outproj_mlp",
    )(x, ya, yb, yc, mod, wl["w_out"], wl["ln1"], wl["w_fc1"], wl["w_fc2"], wl["ln2"])


def _rope_tables(n_tok):
    def angles(rot_dim):
        t = jnp.arange(n_tok, dtype=jnp.int32)
        row, col = (t // GRID_W).astype(F32), (t % GRID_W).astype(F32)
        n_freq = rot_dim // 4
        inv = ROPE_THETA ** (-jnp.arange(n_freq, dtype=F32) / n_freq)
        ang = jnp.concatenate([row[:, None] * inv, col[:, None] * inv], axis=-1)
        return jnp.cos(ang), jnp.sin(ang)

    c64, s64 = angles(HEAD_DIM)
    c32, s32 = angles(C_ROPE_DIM)
    ones = lambda w: jnp.ones((n_tok, w), F32)
    zeros = lambda w: jnp.zeros((n_tok, w), F32)
    cos_a = jnp.tile(jnp.concatenate([c64, c64], axis=-1), (1, A_HEADS))
    sin_a = jnp.tile(jnp.concatenate([-s64, s64], axis=-1), (1, A_HEADS))
    cos_q = jnp.concatenate([ones(C_NOPE_DIM), c32, c32, ones(32)], axis=-1)
    sin_q = jnp.concatenate([zeros(C_NOPE_DIM), -s32, s32, zeros(32)], axis=-1)
    cos_k = jnp.concatenate([c32, c32, ones(96)], axis=-1)
    sin_k = jnp.concatenate([-s32, s32, zeros(96)], axis=-1)
    return cos_a, sin_a, cos_q, sin_q, cos_k, sin_k


def _layer_weights(l, w_in, q_norm_b, k_norm_b, mla_q_norm, mla_kv_norm, w_uq, w_uk, w_uv, w_out,
                   ln1_g, ln1_b, w_fc1, w_fc2, ln2_g, ln2_b):
    uq = w_uq[l].reshape(C_Q_RANK, C_HEADS, C_NOPE_DIM + C_ROPE_DIM)
    uq = jnp.pad(uq, ((0, 0), (0, 0), (0, C_QK_PAD - C_NOPE_DIM - C_ROPE_DIM)))
    uk = w_uk[l].reshape(C_KV_RANK, C_HEADS, C_NOPE_DIM)
    uk = jnp.pad(uk, ((0, 0), (0, 0), (0, C_QK_PAD - C_NOPE_DIM)))
    lane = jnp.arange(256)
    gmat = jnp.where((lane[:, None] // HEAD_DIM) == (lane[None, :] // HEAD_DIM), 1.0 / HEAD_DIM, 0.0)
    return {
        "w_in": jnp.pad(w_in[l], ((0, 0), (0, D_IN_PAD - D_IN))).astype(BF16),
        "w_uq": uq.reshape(C_Q_RANK, C_HEADS * C_QK_PAD).astype(BF16),
        "w_uk": uk.reshape(C_KV_RANK, C_HEADS * C_QK_PAD).astype(BF16),
        "w_uv": w_uv[l].astype(BF16),
        "gq": jnp.tile(q_norm_b[l], B_HEADS)[None, :],
        "gk": jnp.tile(k_norm_b[l], B_KV_HEADS)[None, :],
        "gmat": gmat.astype(BF16),
        "mqn": mla_q_norm[l][None, :],
        "mkvn": mla_kv_norm[l][None, :],
        "w_out": w_out[l].astype(BF16),
        "ln1": jnp.stack([ln1_g[l], ln1_b[l]]),
        "w_fc1": w_fc1[l].astype(BF16),
        "w_fc2": w_fc2[l].astype(BF16),
        "ln2": jnp.stack([ln2_g[l], ln2_b[l]]),
    }


def kernel(x, c, ctx, c_ctx, w_mod, b_mod, w_in, sink_a, q_norm_b, k_norm_b, mla_q_norm, mla_kv_norm,
           w_uq, w_uk, w_uv, w_out, ln1_g, ln1_b, w_fc1, w_fc2, ln2_g, ln2_b):
    bsz, n_tok, _ = x.shape
    n_ctx = ctx.shape[1]
    cc = jnp.concatenate([c, c_ctx[None, :], jnp.zeros((8 - bsz - 1, D_MODEL), F32)], axis=0)
    mods = _modulation(cc, w_mod, b_mod)
    tabs = _rope_tables(n_tok)
    tabs_ctx = tuple(t[:n_ctx] for t in tabs)
    xc = ctx
    for l in range(DEPTH):
        last = l == DEPTH - 1
        wl = _layer_weights(l, w_in, q_norm_b, k_norm_b, mla_q_norm, mla_kv_norm, w_uq, w_uk, w_uv, w_out,
                            ln1_g, ln1_b, w_fc1, w_fc2, ln2_g, ln2_b)
        mod_lat = mods[l, 0:bsz].reshape(bsz, 6, D_MODEL)
        mod_ctx = jnp.broadcast_to(mods[l, bsz].reshape(1, 6, D_MODEL), (bsz, 6, D_MODEL))
        qa, ka, va, qb, kb, vb, qc, kc, vc = _inproj(x, mod_lat, wl, tabs, rope=True, tm=512)
        cx = _inproj(xc, mod_ctx, wl, tabs_ctx, rope=False, tm=n_ctx)
        ya = _window_attn(sink_a[l], qa, cx[1], cx[2], ka, va, tq=256)
        yb = _flash(qb, cx[4], cx[5], kb, vb, shared_kv=True, tq=512, tk=512)
        yc = _flash(qc, cx[7], cx[8], kc, vc, shared_kv=False, tq=512, tk=512)
        x = _mlp(x, ya, yb, yc, mod_lat, wl, tm=512)
        if not last:
            yca, ycb, ycc = _ctx_attn(sink_a[l], cx)
            xc = _mlp(xc, yca, ycb, ycc, mod_ctx, wl, tm=n_ctx)
    return x
```

```python
import functools

import jax
import jax.numpy as jnp
from jax import lax
from jax.experimental import pallas as pl
from jax.experimental.pallas import tpu as pltpu

F32 = jnp.float32
BF16 = jnp.bfloat16

D_MODEL = 1024
DEPTH = 2
GRID_W = 64
HEAD_DIM = 64
WINDOW = 128
A_HEADS, A_KV_HEADS = 4, 2
B_HEADS, B_KV_HEADS = 4, 2
C_HEADS = 8
C_Q_RANK, C_KV_RANK = 256, 128
C_NOPE_DIM, C_ROPE_DIM, C_V_DIM = 64, 32, 64
D_FF = 4 * D_MODEL
ROPE_THETA = 10000.0
NORM_EPS = 1e-6
NEG_INF = -1e30
DEEPNORM_ALPHA = (2 * DEPTH) ** 0.25

OFF_AQ, OFF_AK, OFF_AV = 0, 256, 384
OFF_BQ, OFF_BK, OFF_BV = 512, 768, 896
OFF_CQ, OFF_CKV, OFF_CKR = 1024, 1280, 1408
D_IN = 1440
D_IN_PAD = 1536
LANES = 128
C_QK_PAD = 128
V_PAD = 128
LOG2E = 1.4426950408889634

VMEM_LIMIT = 56 * 1024 * 1024


def _cparams(sem):
    return pltpu.CompilerParams(dimension_semantics=sem, vmem_limit_bytes=VMEM_LIMIT)


def _full(shape):
    n = len(shape)
    return pl.BlockSpec(shape, lambda *_: (0,) * n)


def _mod_kernel(cc_ref, w_ref, b_ref, o_ref):
    cc = cc_ref[...]
    s = cc / (1.0 + jnp.exp(-cc))
    o_ref[0] = jnp.dot(s, w_ref[0], preferred_element_type=F32,
                       precision=lax.Precision.HIGHEST) + b_ref[0]


def _modulation(cc, w_mod, b_mod):
    n_l = w_mod.shape[0]
    tn = 1024
    return pl.pallas_call(
        _mod_kernel,
        grid=(n_l, 6 * D_MODEL // tn),
        in_specs=[pl.BlockSpec((8, D_MODEL), lambda l, j: (0, 0)),
                  pl.BlockSpec((1, D_MODEL, tn), lambda l, j: (l, 0, j)),
                  pl.BlockSpec((1, 1, tn), lambda l, j: (l, 0, j))],
        out_specs=pl.BlockSpec((1, 8, tn), lambda l, j: (l, 0, j)),
        out_shape=jax.ShapeDtypeStruct((n_l, 8, 6 * D_MODEL), F32),
        compiler_params=_cparams(("parallel", "parallel")),
        name="modulation",
    )(cc, w_mod, b_mod.reshape(n_l, 1, 6 * D_MODEL))


def _rope(x, cos, sin_signed, half):
    width = x.shape[-1]
    lane = lax.broadcasted_iota(jnp.int32, x.shape, 1)
    first = (lane % (2 * half)) < half
    nxt = pltpu.roll(x, width - half, axis=1)
    prv = pltpu.roll(x, half, axis=1)
    return x * cos + jnp.where(first, nxt, prv) * sin_signed


def _group_mean_sq(x, g_ref):
    x2 = x * x
    hi = x2.astype(BF16)
    lo = (x2 - hi.astype(F32)).astype(BF16)
    g = g_ref[...]
    return (jnp.dot(hi, g, preferred_element_type=F32) + jnp.dot(lo, g, preferred_element_type=F32))


def _row_rms(x, gain):
    ms = jnp.mean(x * x, axis=-1, keepdims=True)
    return x * lax.rsqrt(ms + NORM_EPS) * gain


def _inproj_kernel(x_ref, mod_ref, w_in_ref, w_uq_ref, w_uk_ref, w_uv_ref,
                   gq_ref, gk_ref, gmat_ref, mqn_ref, mkvn_ref,
                   cos_a_ref, sin_a_ref, cos_q_ref, sin_q_ref, cos_k_ref, sin_k_ref,
                   qa_ref, ka_ref, va_ref, qb_ref, kb_ref, vb_ref, qc_ref, kc_ref, vc_ref,
                   *, rope):
    x = x_ref[0]
    shift, scale = mod_ref[0, 0:1, :], mod_ref[0, 1:2, :]
    h = (x * (1.0 + scale) + shift).astype(BF16)
    p = jnp.dot(h, w_in_ref[...], preferred_element_type=F32)

    def store_heads(ref, val, n_heads, width):
        for hh in range(n_heads):
            ref[0, hh] = val[:, hh * width:(hh + 1) * width].astype(ref.dtype)

    qa = p[:, OFF_AQ:OFF_AQ + 256]
    ka = p[:, OFF_AK:OFF_AK + 128]
    if rope:
        qa = _rope(qa, cos_a_ref[...], sin_a_ref[...], HEAD_DIM // 2)
        ka = _rope(ka, cos_a_ref[:, 0:128], sin_a_ref[:, 0:128], HEAD_DIM // 2)
    store_heads(qa_ref, qa * (HEAD_DIM ** -0.5 * LOG2E), A_HEADS, HEAD_DIM)
    store_heads(ka_ref, ka, A_KV_HEADS, HEAD_DIM)
    store_heads(va_ref, p[:, OFF_AV:OFF_AV + 128], A_KV_HEADS, HEAD_DIM)

    qb = p[:, OFF_BQ:OFF_BQ + 256]
    kb = p[:, OFF_BK:OFF_BK + 128]
    qb = qb * lax.rsqrt(_group_mean_sq(qb, gmat_ref) + NORM_EPS) * gq_ref[...]
    kb = kb * lax.rsqrt(_group_mean_sq(kb, gmat_ref.at[0:128, 0:128]) + NORM_EPS) * gk_ref[...]
    if rope:
        qb = _rope(qb, cos_a_ref[...], sin_a_ref[...], HEAD_DIM // 2)
        kb = _rope(kb, cos_a_ref[:, 0:128], sin_a_ref[:, 0:128], HEAD_DIM // 2)
    store_heads(qb_ref, qb * (HEAD_DIM ** -0.5 * LOG2E), B_HEADS, HEAD_DIM)
    store_heads(kb_ref, kb, B_KV_HEADS, HEAD_DIM)
    vb = p[:, OFF_BV:OFF_BV + 128]
    v_lanes = lax.broadcasted_iota(jnp.int32, vb.shape, 1) < HEAD_DIM
    vb_ref[0, 0] = jnp.where(v_lanes, vb, 1.0).astype(BF16)
    vb_ref[0, 1] = jnp.where(v_lanes, pltpu.roll(vb, HEAD_DIM, axis=1), 1.0).astype(BF16)

    cq = _row_rms(p[:, OFF_CQ:OFF_CQ + C_Q_RANK], mqn_ref[...]).astype(BF16)
    ckv = _row_rms(p[:, OFF_CKV:OFF_CKV + C_KV_RANK], mkvn_ref[...]).astype(BF16)
    q = jnp.dot(cq, w_uq_ref[...], preferred_element_type=F32)
    kn = jnp.dot(ckv, w_uk_ref[...], preferred_element_type=F32)
    vc = jnp.dot(ckv, w_uv_ref[...], preferred_element_type=F32)
    kr = p[:, OFF_CKR:OFF_CKR + LANES]
    if rope:
        kr = _rope(kr, cos_k_ref[...], sin_k_ref[...], C_ROPE_DIM // 2)
    kr = pltpu.roll(kr, C_NOPE_DIM, axis=1)
    c_scale = (C_NOPE_DIM + C_ROPE_DIM) ** -0.5 * LOG2E
    for hh in range(C_HEADS):
        qh = q[:, hh * C_QK_PAD:(hh + 1) * C_QK_PAD]
        if rope:
            qh = _rope(qh, cos_q_ref[...], sin_q_ref[...], C_ROPE_DIM // 2)
        qc_ref[0, hh] = (qh * c_scale).astype(BF16)
        kc_ref[0, hh] = (kn[:, hh * C_QK_PAD:(hh + 1) * C_QK_PAD] + kr).astype(BF16)
        vc_ref[0, hh] = jnp.where(v_lanes, vc[:, hh * V_PAD:(hh + 1) * V_PAD], 1.0).astype(BF16)


def _inproj(x, mod, wl, tabs, *, rope, tm):
    bsz, n_tok, _ = x.shape
    nt = n_tok // tm
    tok = lambda width: pl.BlockSpec((tm, width), lambda i, b: (i, 0))
    heads = lambda n, width: pl.BlockSpec((1, n, tm, width), lambda i, b: (b, 0, i, 0))
    hshape = lambda n, width: jax.ShapeDtypeStruct((bsz, n, n_tok, width), BF16)
    return pl.pallas_call(
        functools.partial(_inproj_kernel, rope=rope),
        grid=(nt, bsz),
        in_specs=[pl.BlockSpec((1, tm, D_MODEL), lambda i, b: (b, i, 0)),
                  pl.BlockSpec((1, 6, D_MODEL), lambda i, b: (b, 0, 0)),
                  _full((D_MODEL, D_IN_PAD)), _full((C_Q_RANK, C_HEADS * C_QK_PAD)),
                  _full((C_KV_RANK, C_HEADS * C_QK_PAD)), _full((C_KV_RANK, C_HEADS * V_PAD)),
                  _full((1, 256)), _full((1, 128)), _full((256, 256)),
                  _full((1, C_Q_RANK)), _full((1, C_KV_RANK)),
                  tok(256), tok(256), tok(128), tok(128), tok(128), tok(128)],
        out_specs=[heads(A_HEADS, HEAD_DIM), heads(A_KV_HEADS, HEAD_DIM), heads(A_KV_HEADS, HEAD_DIM),
                   heads(B_HEADS, HEAD_DIM), heads(B_KV_HEADS, HEAD_DIM), heads(B_KV_HEADS, V_PAD),
                   heads(C_HEADS, C_QK_PAD), heads(C_HEADS, C_QK_PAD), heads(C_HEADS, V_PAD)],
        out_shape=[hshape(A_HEADS, HEAD_DIM), hshape(A_KV_HEADS, HEAD_DIM), hshape(A_KV_HEADS, HEAD_DIM),
                   hshape(B_HEADS, HEAD_DIM), hshape(B_KV_HEADS, HEAD_DIM), hshape(B_KV_HEADS, V_PAD),
                   hshape(C_HEADS, C_QK_PAD), hshape(C_HEADS, C_QK_PAD), hshape(C_HEADS, V_PAD)],
        compiler_params=_cparams(("parallel", "parallel")),
        name="inproj_rope" if rope else "inproj_ctx",
    )(x, mod, wl["w_in"], wl["w_uq"], wl["w_uk"], wl["w_uv"], wl["gq"], wl["gk"], wl["gmat"],
      wl["mqn"], wl["mkvn"], *tabs)


def _qk(q, k):
    return lax.dot_general(q, k, (((1,), (1,)), ((), ())), preferred_element_type=F32)


def _online_step(q, k, v, m, acc):
    s = _qk(q, k)
    m_new = jnp.maximum(m, jnp.max(s, axis=-1, keepdims=True))
    p = jnp.exp2(s - m_new)
    acc = jnp.exp2(m - m_new) * acc + jnp.dot(p.astype(v.dtype), v, preferred_element_type=F32)
    return m_new, acc


def _softmax_attend(q, k, v, sink=None):
    s = _qk(q, k)
    m = jnp.max(s, axis=-1, keepdims=True)
    if sink is not None:
        m = jnp.maximum(m, sink)
    p = jnp.exp2(s - m)
    l = jnp.sum(p, axis=-1, keepdims=True)
    if sink is not None:
        l = l + jnp.exp2(sink - m)
    return jnp.dot(p.astype(v.dtype), v, preferred_element_type=F32) / l


def _flash_kernel(q_ref, kc_ref, vc_ref, kl_ref, vl_ref, o_ref, *, shared_kv, tk, unroll):
    tq = q_ref.shape[2]
    n_chunks = kl_ref.shape[2] // tk
    dv = o_ref.shape[2] // 2
    heads = ((q_ref[0, 0], 0), (q_ref[0, 1], 0 if shared_kv else 1))

    def step(carry, kv):
        return tuple(_online_step(q, *kv(hk), *c) for (q, hk), c in zip(heads, carry))

    init = tuple((jnp.full((tq, 1), NEG_INF, F32), jnp.zeros((tq, vl_ref.shape[3]), F32)) for _ in heads)
    carry = step(init, lambda hk: (kc_ref[0, hk], vc_ref[0, hk]))

    def body(j, c):
        off = pl.multiple_of(j * tk, tk)
        return step(c, lambda hk: (kl_ref[0, hk, pl.ds(off, tk), :], vl_ref[0, hk, pl.ds(off, tk), :]))

    carry = lax.fori_loop(0, n_chunks, body, carry, unroll=unroll)
    outs = [(acc / pltpu.roll(acc, dv, axis=1))[:, :dv] for _, acc in carry]
    o_ref[0] = jnp.concatenate(outs, axis=-1).astype(o_ref.dtype)


def _flash(q, k_ctx, v_ctx, k_lat, v_lat, *, shared_kv, tq, tk, unroll):
    bsz, n_heads, n_tok, dk = q.shape
    n_ctx, dvp = k_ctx.shape[2], v_lat.shape[3]
    dv = dvp // 2
    kvb = 1 if shared_kv else 2
    kv_spec = lambda n, width: pl.BlockSpec((1, kvb, n, width), lambda b, hp, i: (b, hp, 0, 0))
    return pl.pallas_call(
        functools.partial(_flash_kernel, shared_kv=shared_kv, tk=tk, unroll=unroll),
        grid=(bsz, n_heads // 2, n_tok // tq),
        in_specs=[pl.BlockSpec((1, 2, tq, dk), lambda b, hp, i: (b, hp, i, 0)),
                  kv_spec(n_ctx, dk), kv_spec(n_ctx, dvp), kv_spec(n_tok, dk), kv_spec(n_tok, dvp)],
        out_specs=pl.BlockSpec((1, tq, 2 * dv), lambda b, hp, i: (b, i, hp)),
        out_shape=jax.ShapeDtypeStruct((bsz, n_tok, n_heads * dv), BF16),
        compiler_params=_cparams(("parallel", "parallel", "arbitrary")),
        name="flash_gqa" if shared_kv else "flash_mla",
    )(q, k_ctx, v_ctx, k_lat, v_lat)


def _window_kernel(sink_ref, q_ref, kc_ref, vc_ref, kl_ref, vl_ref, o_ref):
    tq = q_ref.shape[2]
    n_tok = kl_ref.shape[2]
    span = tq + 2 * WINDOW
    kvh, i = pl.program_id(1), pl.program_id(2)
    q0 = i * tq
    start = pl.multiple_of(jnp.clip(q0 - WINDOW, 0, n_tok - span), WINDOW)
    q = q_ref[0].reshape(2 * tq, q_ref.shape[3])
    k_loc = kl_ref[0, 0, pl.ds(start, span), :]
    v_loc = vl_ref[0, 0, pl.ds(start, span), :]

    s_loc = _qk(q, k_loc)
    row = lax.broadcasted_iota(jnp.int32, s_loc.shape, 0)
    col = lax.broadcasted_iota(jnp.int32, s_loc.shape, 1)
    q_pos = q0 + jnp.where(row >= tq, row - tq, row)
    rel = (start + col) - q_pos
    s_loc = jnp.where(jnp.abs(rel) <= WINDOW, s_loc, NEG_INF)
    s_ctx = _qk(q, kc_ref[0, 0])

    row1 = lax.broadcasted_iota(jnp.int32, (2 * tq, 1), 0)
    sink = jnp.where(row1 < tq, sink_ref[2 * kvh], sink_ref[2 * kvh + 1]) * LOG2E
    m = jnp.maximum(jnp.maximum(jnp.max(s_loc, axis=-1, keepdims=True),
                                jnp.max(s_ctx, axis=-1, keepdims=True)), sink)
    p_loc = jnp.exp2(s_loc - m)
    p_ctx = jnp.exp2(s_ctx - m)
    l = (jnp.sum(p_loc, axis=-1, keepdims=True) + jnp.sum(p_ctx, axis=-1, keepdims=True)
         + jnp.exp2(sink - m))
    o = (jnp.dot(p_loc.astype(BF16), v_loc, preferred_element_type=F32)
         + jnp.dot(p_ctx.astype(BF16), vc_ref[0, 0], preferred_element_type=F32)) / l
    o_ref[0] = jnp.concatenate([o[:tq], o[tq:]], axis=-1).astype(o_ref.dtype)


def _window_attn(sink, q, k_ctx, v_ctx, k_lat, v_lat, *, tq):
    bsz, n_heads, n_tok, d = q.shape
    n_ctx = k_ctx.shape[2]
    kv_spec = lambda n: pl.BlockSpec((1, 1, n, d), lambda b, kvh, i: (b, kvh, 0, 0))
    return pl.pallas_call(
        _window_kernel,
        grid=(bsz, n_heads // 2, n_tok // tq),
        in_specs=[pl.BlockSpec(memory_space=pltpu.SMEM),
                  pl.BlockSpec((1, 2, tq, d), lambda b, kvh, i: (b, kvh, i, 0)),
                  kv_spec(n_ctx), kv_spec(n_ctx), kv_spec(n_tok), kv_spec(n_tok)],
        out_specs=pl.BlockSpec((1, tq, 2 * d), lambda b, kvh, i: (b, i, kvh)),
        out_shape=jax.ShapeDtypeStruct((bsz, n_tok, n_heads * d), BF16),
        compiler_params=_cparams(("parallel", "parallel", "arbitrary")),
        name="window_gqa",
    )(sink, q, k_ctx, v_ctx, k_lat, v_lat)


def _ctx_attn_kernel(sink_ref, qa_ref, ka_ref, va_ref, qb_ref, kb_ref, vb_ref, qc_ref, kc_ref, vc_ref,
                     ya_ref, yb_ref, yc_ref):
    n = qa_ref.shape[2]
    row1 = lax.broadcasted_iota(jnp.int32, (2 * n, 1), 0)

    def gqa(q_ref, k_ref, v_ref, y_ref, with_sink):
        outs = []
        for kvh in range(k_ref.shape[1]):
            q = q_ref[0, 2 * kvh:2 * kvh + 2].reshape(2 * n, q_ref.shape[3])
            sink = (jnp.where(row1 < n, sink_ref[2 * kvh], sink_ref[2 * kvh + 1]) * LOG2E
                    if with_sink else None)
            o = _softmax_attend(q, k_ref[0, kvh], v_ref[0, kvh, :, 0:HEAD_DIM], sink)
            outs += [o[:n], o[n:]]
        y_ref[0] = jnp.concatenate(outs, axis=-1).astype(y_ref.dtype)

    gqa(qa_ref, ka_ref, va_ref, ya_ref, True)
    gqa(qb_ref, kb_ref, vb_ref, yb_ref, False)
    outs = [_softmax_attend(qc_ref[0, hh], kc_ref[0, hh], vc_ref[0, hh, :, 0:C_V_DIM]) for hh in range(C_HEADS)]
    yc_ref[0] = jnp.concatenate(outs, axis=-1).astype(yc_ref.dtype)


def _ctx_attn(sink, cx):
    bsz, _, n, _ = cx[0].shape
    spec = lambda a: pl.BlockSpec((1,) + a.shape[1:], lambda b: (b, 0, 0, 0))
    widths = (A_HEADS * HEAD_DIM, B_HEADS * HEAD_DIM, C_HEADS * C_V_DIM)
    return pl.pallas_call(
        _ctx_attn_kernel,
        grid=(bsz,),
        in_specs=[pl.BlockSpec(memory_space=pltpu.SMEM)] + [spec(a) for a in cx],
        out_specs=[pl.BlockSpec((1, n, w), lambda b: (b, 0, 0)) for w in widths],
        out_shape=[jax.ShapeDtypeStruct((bsz, n, w), BF16) for w in widths],
        compiler_params=_cparams(("parallel",)),
        name="ctx_attn",
    )(sink, *cx)


def _layer_norm(v, g, b):
    mu = jnp.mean(v, axis=-1, keepdims=True)
    d = v - mu
    var = jnp.mean(d * d, axis=-1, keepdims=True)
    return d * lax.rsqrt(var + NORM_EPS) * g + b


def _mlp_kernel(x_ref, ya_ref, yb_ref, yc_ref, mod_ref, w_out_ref, ln1_ref, w1_ref, w2_ref, ln2_ref,
                o_ref, *, ff_chunk):
    x = x_ref[0]
    g1, sh2, sc2, g2 = (mod_ref[0, r:r + 1, :] for r in (2, 3, 4, 5))
    y = (jnp.dot(ya_ref[0], w_out_ref[0:256, :], preferred_element_type=F32)
         + jnp.dot(yb_ref[0], w_out_ref[256:512, :], preferred_element_type=F32)
         + jnp.dot(yc_ref[0], w_out_ref[512:1024, :], preferred_element_type=F32))
    x1 = _layer_norm(DEEPNORM_ALPHA * x + g1 * y, ln1_ref[0:1, :], ln1_ref[1:2, :])
    h = (x1 * (1.0 + sc2) + sh2).astype(BF16)
    acc = jnp.zeros(x.shape, F32)
    for c0 in range(0, D_FF, ff_chunk):
        a = jnp.maximum(jnp.dot(h, w1_ref[:, c0:c0 + ff_chunk], preferred_element_type=F32), 0.0)
        acc = acc + jnp.dot((a * a).astype(BF16), w2_ref[c0:c0 + ff_chunk, :], preferred_element_type=F32)
    o_ref[0] = _layer_norm(DEEPNORM_ALPHA * x1 + g2 * acc, ln2_ref[0:1, :], ln2_ref[1:2, :])


def _mlp(x, ya, yb, yc, mod, wl, *, tm):
    bsz, n_tok, _ = x.shape
    tok = lambda width: pl.BlockSpec((1, tm, width), lambda b, i: (b, i, 0))
    const = lambda shape: pl.BlockSpec(shape, lambda b, i: (0, 0), pipeline_mode=pl.Buffered(1))
    return pl.pallas_call(
        functools.partial(_mlp_kernel, ff_chunk=1024),
        grid=(bsz, n_tok // tm),
        in_specs=[tok(D_MODEL), tok(256), tok(256), tok(512),
                  pl.BlockSpec((1, 6, D_MODEL), lambda b, i: (b, 0, 0)),
                  const((D_MODEL, D_MODEL)), const((2, D_MODEL)),
                  const((D_MODEL, D_FF)), const((D_FF, D_MODEL)), const((2, D_MODEL))],
        out_specs=tok(D_MODEL),
        out_shape=jax.ShapeDtypeStruct(x.shape, F32),
        compiler_params=_cparams(("parallel", "parallel")),
        name="outproj_mlp",
    )(x, ya, yb, yc, mod, wl["w_out"], wl["ln1"], wl["w_fc1"], wl["w_fc2"], wl["ln2"])


def _rope_tables(n_tok):
    def angles(rot_dim):
        t = jnp.arange(n_tok, dtype=jnp.int32)
        row, col = (t // GRID_W).astype(F32), (t % GRID_W).astype(F32)
        n_freq = rot_dim // 4
        inv = ROPE_THETA ** (-jnp.arange(n_freq, dtype=F32) / n_freq)
        ang = jnp.concatenate([row[:, None] * inv, col[:, None] * inv], axis=-1)
        return jnp.cos(ang), jnp.sin(ang)

    c64, s64 = angles(HEAD_DIM)
    c32, s32 = angles(C_ROPE_DIM)
    ones = lambda w: jnp.ones((n_tok, w), F32)
    zeros = lambda w: jnp.zeros((n_tok, w), F32)
    cos_a = jnp.tile(jnp.concatenate([c64, c64], axis=-1), (1, A_HEADS))
    sin_a = jnp.tile(jnp.concatenate([-s64, s64], axis=-1), (1, A_HEADS))
    cos_q = jnp.concatenate([ones(C_NOPE_DIM), c32, c32, ones(32)], axis=-1)
    sin_q = jnp.concatenate([zeros(C_NOPE_DIM), -s32, s32, zeros(32)], axis=-1)
    cos_k = jnp.concatenate([c32, c32, ones(96)], axis=-1)
    sin_k = jnp.concatenate([-s32, s32, zeros(96)], axis=-1)
    return cos_a, sin_a, cos_q, sin_q, cos_k, sin_k


def _layer_weights(l, w_in, q_norm_b, k_norm_b, mla_q_norm, mla_kv_norm, w_uq, w_uk, w_uv, w_out,
                   ln1_g, ln1_b, w_fc1, w_fc2, ln2_g, ln2_b):
    uq = w_uq[l].reshape(C_Q_RANK, C_HEADS, C_NOPE_DIM + C_ROPE_DIM)
    uq = jnp.pad(uq, ((0, 0), (0, 0), (0, C_QK_PAD - C_NOPE_DIM - C_ROPE_DIM)))
    uk = w_uk[l].reshape(C_KV_RANK, C_HEADS, C_NOPE_DIM)
    uk = jnp.pad(uk, ((0, 0), (0, 0), (0, C_QK_PAD - C_NOPE_DIM)))
    uv = w_uv[l].reshape(C_KV_RANK, C_HEADS, C_V_DIM)
    uv = jnp.pad(uv, ((0, 0), (0, 0), (0, V_PAD - C_V_DIM)))
    lane = jnp.arange(256)
    gmat = jnp.where((lane[:, None] // HEAD_DIM) == (lane[None, :] // HEAD_DIM), 1.0 / HEAD_DIM, 0.0)
    return {
        "w_in": jnp.pad(w_in[l], ((0, 0), (0, D_IN_PAD - D_IN))).astype(BF16),
        "w_uq": uq.reshape(C_Q_RANK, C_HEADS * C_QK_PAD).astype(BF16),
        "w_uk": uk.reshape(C_KV_RANK, C_HEADS * C_QK_PAD).astype(BF16),
        "w_uv": uv.reshape(C_KV_RANK, C_HEADS * V_PAD).astype(BF16),
        "gq": jnp.tile(q_norm_b[l], B_HEADS)[None, :],
        "gk": jnp.tile(k_norm_b[l], B_KV_HEADS)[None, :],
        "gmat": gmat.astype(BF16),
        "mqn": mla_q_norm[l][None, :],
        "mkvn": mla_kv_norm[l][None, :],
        "w_out": w_out[l].astype(BF16),
        "ln1": jnp.stack([ln1_g[l], ln1_b[l]]),
        "w_fc1": w_fc1[l].astype(BF16),
        "w_fc2": w_fc2[l].astype(BF16),
        "ln2": jnp.stack([ln2_g[l], ln2_b[l]]),
    }


def kernel(x, c, ctx, c_ctx, w_mod, b_mod, w_in, sink_a, q_norm_b, k_norm_b, mla_q_norm, mla_kv_norm,
           w_uq, w_uk, w_uv, w_out, ln1_g, ln1_b, w_fc1, w_fc2, ln2_g, ln2_b):
    bsz, n_tok, _ = x.shape
    n_ctx = ctx.shape[1]
    cc = jnp.concatenate([c, c_ctx[None, :], jnp.zeros((8 - bsz - 1, D_MODEL), F32)], axis=0)
    mods = _modulation(cc, w_mod, b_mod)
    tabs = _rope_tables(n_tok)
    tabs_ctx = tuple(t[:n_ctx] for t in tabs)
    xc = ctx
    for l in range(DEPTH):
        last = l == DEPTH - 1
        wl = _layer_weights(l, w_in, q_norm_b, k_norm_b, mla_q_norm, mla_kv_norm, w_uq, w_uk, w_uv, w_out,
                            ln1_g, ln1_b, w_fc1, w_fc2, ln2_g, ln2_b)
        mod_lat = mods[l, 0:bsz].reshape(bsz, 6, D_MODEL)
        mod_ctx = jnp.broadcast_to(mods[l, bsz].reshape(1, 6, D_MODEL), (bsz, 6, D_MODEL))
        qa, ka, va, qb, kb, vb, qc, kc, vc = _inproj(x, mod_lat, wl, tabs, rope=True, tm=512)
        cx = _inproj(xc, mod_ctx, wl, tabs_ctx, rope=False, tm=n_ctx)
        ya = _window_attn(sink_a[l], qa, cx[1], cx[2], ka, va, tq=256)
        yb = _flash(qb, cx[4], cx[5], kb, vb, shared_kv=True, tq=1024, tk=512, unroll=4)
        yc = _flash(qc, cx[7], cx[8], kc, vc, shared_kv=False, tq=1024, tk=512, unroll=4)
        x = _mlp(x, ya, yb, yc, mod_lat, wl, tm=512)
        if not last:
            yca, ycb, ycc = _ctx_attn(sink_a[l], cx)
            xc = _mlp(xc, yca, ycb, ycc, mod_ctx, wl, tm=n_ctx)
    return x
```

```python
import functools

import jax
import jax.numpy as jnp
from jax import lax
from jax.experimental import pallas as pl
from jax.experimental.pallas import tpu as pltpu

F32 = jnp.float32
BF16 = jnp.bfloat16

D_MODEL = 1024
DEPTH = 2
GRID_W = 64
HEAD_DIM = 64
WINDOW = 128
A_HEADS, A_KV_HEADS = 4, 2
B_HEADS, B_KV_HEADS = 4, 2
C_HEADS = 8
C_Q_RANK, C_KV_RANK = 256, 128
C_NOPE_DIM, C_ROPE_DIM, C_V_DIM = 64, 32, 64
D_FF = 4 * D_MODEL
ROPE_THETA = 10000.0
NORM_EPS = 1e-6
NEG_INF = -1e30
DEEPNORM_ALPHA = (2 * DEPTH) ** 0.25

OFF_AQ, OFF_AK, OFF_AV = 0, 256, 384
OFF_BQ, OFF_BK, OFF_BV = 512, 768, 896
OFF_CQ, OFF_CKV, OFF_CKR = 1024, 1280, 1408
D_IN = 1440
D_IN_PAD = 1536
LANES = 128
C_QK_PAD = 128
V_PAD = 128
LOG2E = 1.4426950408889634

VMEM_LIMIT = 56 * 1024 * 1024


def _cparams(sem):
    return pltpu.CompilerParams(dimension_semantics=sem, vmem_limit_bytes=VMEM_LIMIT)


def _full(shape):
    n = len(shape)
    return pl.BlockSpec(shape, lambda *_: (0,) * n)


def _mod_kernel(cc_ref, w_ref, b_ref, o_ref):
    cc = cc_ref[...]
    s = cc / (1.0 + jnp.exp(-cc))
    o_ref[0] = jnp.dot(s, w_ref[0], preferred_element_type=F32,
                       precision=lax.Precision.HIGHEST) + b_ref[0]


def _modulation(cc, w_mod, b_mod):
    n_l = w_mod.shape[0]
    tn = 1024
    return pl.pallas_call(
        _mod_kernel,
        grid=(n_l, 6 * D_MODEL // tn),
        in_specs=[pl.BlockSpec((8, D_MODEL), lambda l, j: (0, 0)),
                  pl.BlockSpec((1, D_MODEL, tn), lambda l, j: (l, 0, j)),
                  pl.BlockSpec((1, 1, tn), lambda l, j: (l, 0, j))],
        out_specs=pl.BlockSpec((1, 8, tn), lambda l, j: (l, 0, j)),
        out_shape=jax.ShapeDtypeStruct((n_l, 8, 6 * D_MODEL), F32),
        compiler_params=_cparams(("parallel", "parallel")),
        name="modulation",
    )(cc, w_mod, b_mod.reshape(n_l, 1, 6 * D_MODEL))


def _rope(x, cos, sin_signed, half):
    width = x.shape[-1]
    lane = lax.broadcasted_iota(jnp.int32, x.shape, 1)
    first = (lane % (2 * half)) < half
    nxt = pltpu.roll(x, width - half, axis=1)
    prv = pltpu.roll(x, half, axis=1)
    return x * cos + jnp.where(first, nxt, prv) * sin_signed


def _group_mean_sq(x, g_ref):
    x2 = x * x
    hi = x2.astype(BF16)
    lo = (x2 - hi.astype(F32)).astype(BF16)
    g = g_ref[...]
    return (jnp.dot(hi, g, preferred_element_type=F32) + jnp.dot(lo, g, preferred_element_type=F32))


def _row_rms(x, gain):
    ms = jnp.mean(x * x, axis=-1, keepdims=True)
    return x * lax.rsqrt(ms + NORM_EPS) * gain


N_INPROJ_IN = 17
MERGED_OUTS = (4, 5, 7, 8)


def _inproj_kernel(*refs, rope):
    (x_ref, mod_ref, w_in_ref, w_uq_ref, w_uk_ref, w_uv_ref, gq_ref, gk_ref, gmat_ref, mqn_ref, mkvn_ref,
     cos_a_ref, sin_a_ref, cos_q_ref, sin_q_ref, cos_k_ref, sin_k_ref) = refs[:N_INPROJ_IN]
    qa_ref, ka_ref, va_ref, qb_ref, kb_ref, vb_ref, qc_ref, kc_ref, vc_ref = refs[-9:]
    x = x_ref[0]
    shift, scale = mod_ref[0, 0:1, :], mod_ref[0, 1:2, :]
    h = (x * (1.0 + scale) + shift).astype(BF16)
    p = jnp.dot(h, w_in_ref[...], preferred_element_type=F32)

    def store_heads(ref, val, n_heads, width):
        for hh in range(n_heads):
            ref[0, hh] = val[:, hh * width:(hh + 1) * width].astype(ref.dtype)

    qa = p[:, OFF_AQ:OFF_AQ + 256]
    ka = p[:, OFF_AK:OFF_AK + 128]
    if rope:
        qa = _rope(qa, cos_a_ref[...], sin_a_ref[...], HEAD_DIM // 2)
        ka = _rope(ka, cos_a_ref[:, 0:128], sin_a_ref[:, 0:128], HEAD_DIM // 2)
    store_heads(qa_ref, qa * (HEAD_DIM ** -0.5 * LOG2E), A_HEADS, HEAD_DIM)
    store_heads(ka_ref, ka, A_KV_HEADS, HEAD_DIM)
    store_heads(va_ref, p[:, OFF_AV:OFF_AV + 128], A_KV_HEADS, HEAD_DIM)

    qb = p[:, OFF_BQ:OFF_BQ + 256]
    kb = p[:, OFF_BK:OFF_BK + 128]
    qb = qb * lax.rsqrt(_group_mean_sq(qb, gmat_ref) + NORM_EPS) * gq_ref[...]
    kb = kb * lax.rsqrt(_group_mean_sq(kb, gmat_ref.at[0:128, 0:128]) + NORM_EPS) * gk_ref[...]
    if rope:
        qb = _rope(qb, cos_a_ref[...], sin_a_ref[...], HEAD_DIM // 2)
        kb = _rope(kb, cos_a_ref[:, 0:128], sin_a_ref[:, 0:128], HEAD_DIM // 2)
    store_heads(qb_ref, qb * (HEAD_DIM ** -0.5 * LOG2E), B_HEADS, HEAD_DIM)
    store_heads(kb_ref, kb, B_KV_HEADS, HEAD_DIM)
    vb = p[:, OFF_BV:OFF_BV + 128]
    v_lanes = lax.broadcasted_iota(jnp.int32, vb.shape, 1) < HEAD_DIM
    vb_ref[0, 0] = jnp.where(v_lanes, vb, 1.0).astype(BF16)
    vb_ref[0, 1] = jnp.where(v_lanes, pltpu.roll(vb, HEAD_DIM, axis=1), 1.0).astype(BF16)

    cq = _row_rms(p[:, OFF_CQ:OFF_CQ + C_Q_RANK], mqn_ref[...]).astype(BF16)
    ckv = _row_rms(p[:, OFF_CKV:OFF_CKV + C_KV_RANK], mkvn_ref[...]).astype(BF16)
    q = jnp.dot(cq, w_uq_ref[...], preferred_element_type=F32)
    kn = jnp.dot(ckv, w_uk_ref[...], preferred_element_type=F32)
    vc = jnp.dot(ckv, w_uv_ref[...], preferred_element_type=F32)
    kr = p[:, OFF_CKR:OFF_CKR + LANES]
    if rope:
        kr = _rope(kr, cos_k_ref[...], sin_k_ref[...], C_ROPE_DIM // 2)
    kr = pltpu.roll(kr, C_NOPE_DIM, axis=1)
    c_scale = (C_NOPE_DIM + C_ROPE_DIM) ** -0.5 * LOG2E
    for hh in range(C_HEADS):
        qh = q[:, hh * C_QK_PAD:(hh + 1) * C_QK_PAD]
        if rope:
            qh = _rope(qh, cos_q_ref[...], sin_q_ref[...], C_ROPE_DIM // 2)
        qc_ref[0, hh] = (qh * c_scale).astype(BF16)
        kc_ref[0, hh] = (kn[:, hh * C_QK_PAD:(hh + 1) * C_QK_PAD] + kr).astype(BF16)
        vc_ref[0, hh] = jnp.where(v_lanes, vc[:, hh * V_PAD:(hh + 1) * V_PAD], 1.0).astype(BF16)


def _inproj(x, mod, wl, tabs, *, rope, tm, n_all, merged=()):
    bsz, n_tok, _ = x.shape
    nt = n_tok // tm
    row_off = (n_all - n_tok) // tm if merged else 0
    out_dims = ((A_HEADS, HEAD_DIM), (A_KV_HEADS, HEAD_DIM), (A_KV_HEADS, HEAD_DIM),
                (B_HEADS, HEAD_DIM), (B_KV_HEADS, HEAD_DIM), (B_KV_HEADS, V_PAD),
                (C_HEADS, C_QK_PAD), (C_HEADS, C_QK_PAD), (C_HEADS, V_PAD))
    tok = lambda width: pl.BlockSpec((tm, width), lambda i, b: (i, 0))

    def out_spec(o, n, width):
        off = row_off if o in MERGED_OUTS else 0
        return pl.BlockSpec((1, n, tm, width), lambda i, b: (b, 0, i + off, 0))

    def out_shape(o, n, width):
        return jax.ShapeDtypeStruct((bsz, n, n_all if o in MERGED_OUTS else n_tok, width), BF16)

    return pl.pallas_call(
        functools.partial(_inproj_kernel, rope=rope),
        grid=(nt, bsz),
        in_specs=[pl.BlockSpec((1, tm, D_MODEL), lambda i, b: (b, i, 0)),
                  pl.BlockSpec((1, 6, D_MODEL), lambda i, b: (b, 0, 0)),
                  _full((D_MODEL, D_IN_PAD)), _full((C_Q_RANK, C_HEADS * C_QK_PAD)),
                  _full((C_KV_RANK, C_HEADS * C_QK_PAD)), _full((C_KV_RANK, C_HEADS * V_PAD)),
                  _full((1, 256)), _full((1, 128)), _full((256, 256)),
                  _full((1, C_Q_RANK)), _full((1, C_KV_RANK)),
                  tok(256), tok(256), tok(128), tok(128), tok(128), tok(128)]
                 + [pl.BlockSpec(memory_space=pl.ANY)] * len(merged),
        out_specs=[out_spec(o, *d) for o, d in enumerate(out_dims)],
        out_shape=[out_shape(o, *d) for o, d in enumerate(out_dims)],
        input_output_aliases={N_INPROJ_IN + j: o for j, o in enumerate(MERGED_OUTS)} if merged else {},
        compiler_params=_cparams(("parallel", "parallel")),
        name="inproj_rope" if rope else "inproj_ctx",
    )(x, mod, wl["w_in"], wl["w_uq"], wl["w_uk"], wl["w_uv"], wl["gq"], wl["gk"], wl["gmat"],
      wl["mqn"], wl["mkvn"], *tabs, *merged)


def _qk(q, k):
    return lax.dot_general(q, k, (((1,), (1,)), ((), ())), preferred_element_type=F32)


def _online_step(q, k, v, m, acc):
    s = _qk(q, k)
    m_new = jnp.maximum(m, jnp.max(s, axis=-1, keepdims=True))
    p = jnp.exp2(s - m_new)
    acc = jnp.exp2(m - m_new) * acc + jnp.dot(p.astype(v.dtype), v, preferred_element_type=F32)
    return m_new, acc


def _softmax_attend(q, k, v, sink=None):
    s = _qk(q, k)
    m = jnp.max(s, axis=-1, keepdims=True)
    if sink is not None:
        m = jnp.maximum(m, sink)
    p = jnp.exp2(s - m)
    l = jnp.sum(p, axis=-1, keepdims=True)
    if sink is not None:
        l = l + jnp.exp2(sink - m)
    return jnp.dot(p.astype(v.dtype), v, preferred_element_type=F32) / l


def _flash_kernel(q_ref, k_ref, v_ref, o_ref, *, shared_kv, tk):
    tq = q_ref.shape[2]
    dv = o_ref.shape[2] // 2
    heads = ((q_ref[0, 0], 0), (q_ref[0, 1], 0 if shared_kv else 1))
    carry = [(jnp.full((tq, 1), NEG_INF, F32), jnp.zeros((tq, v_ref.shape[3]), F32)) for _ in heads]
    for off in range(0, k_ref.shape[2], tk):
        carry = [_online_step(q, k_ref[0, hk, off:off + tk, :], v_ref[0, hk, off:off + tk, :], *c)
                 for (q, hk), c in zip(heads, carry)]
    outs = [(acc / pltpu.roll(acc, dv, axis=1))[:, :dv] for _, acc in carry]
    o_ref[0] = jnp.concatenate(outs, axis=-1).astype(o_ref.dtype)


def _flash(q, k, v, *, shared_kv, tq, tk):
    bsz, n_heads, n_tok, dk = q.shape
    n_keys, dvp = v.shape[2], v.shape[3]
    dv = dvp // 2
    kvb = 1 if shared_kv else 2
    kv_spec = lambda width: pl.BlockSpec((1, kvb, n_keys, width), lambda b, hp, i: (b, hp, 0, 0))
    return pl.pallas_call(
        functools.partial(_flash_kernel, shared_kv=shared_kv, tk=tk),
        grid=(bsz, n_heads // 2, n_tok // tq),
        in_specs=[pl.BlockSpec((1, 2, tq, dk), lambda b, hp, i: (b, hp, i, 0)), kv_spec(dk), kv_spec(dvp)],
        out_specs=pl.BlockSpec((1, tq, 2 * dv), lambda b, hp, i: (b, i, hp)),
        out_shape=jax.ShapeDtypeStruct((bsz, n_tok, n_heads * dv), BF16),
        compiler_params=_cparams(("parallel", "parallel", "arbitrary")),
        name="flash_gqa" if shared_kv else "flash_mla",
    )(q, k, v)


def _window_kernel(sink_ref, q_ref, kc_ref, vc_ref, kl_ref, vl_ref, o_ref):
    tq = q_ref.shape[2]
    n_tok = kl_ref.shape[2]
    span = tq + 2 * WINDOW
    kvh, i = pl.program_id(1), pl.program_id(2)
    q0 = i * tq
    start = pl.multiple_of(jnp.clip(q0 - WINDOW, 0, n_tok - span), WINDOW)
    q = q_ref[0].reshape(2 * tq, q_ref.shape[3])
    k_loc = kl_ref[0, 0, pl.ds(start, span), :]
    v_loc = vl_ref[0, 0, pl.ds(start, span), :]

    s_loc = _qk(q, k_loc)
    row = lax.broadcasted_iota(jnp.int32, s_loc.shape, 0)
    col = lax.broadcasted_iota(jnp.int32, s_loc.shape, 1)
    q_pos = q0 + jnp.where(row >= tq, row - tq, row)
    rel = (start + col) - q_pos
    s_loc = jnp.where(jnp.abs(rel) <= WINDOW, s_loc, NEG_INF)
    s_ctx = _qk(q, kc_ref[0, 0])

    row1 = lax.broadcasted_iota(jnp.int32, (2 * tq, 1), 0)
    sink = jnp.where(row1 < tq, sink_ref[2 * kvh], sink_ref[2 * kvh + 1]) * LOG2E
    m = jnp.maximum(jnp.maximum(jnp.max(s_loc, axis=-1, keepdims=True),
                                jnp.max(s_ctx, axis=-1, keepdims=True)), sink)
    p_loc = jnp.exp2(s_loc - m)
    p_ctx = jnp.exp2(s_ctx - m)
    l = (jnp.sum(p_loc, axis=-1, keepdims=True) + jnp.sum(p_ctx, axis=-1, keepdims=True)
         + jnp.exp2(sink - m))
    o = (jnp.dot(p_loc.astype(BF16), v_loc, preferred_element_type=F32)
         + jnp.dot(p_ctx.astype(BF16), vc_ref[0, 0], preferred_element_type=F32)) / l
    o_ref[0] = jnp.concatenate([o[:tq], o[tq:]], axis=-1).astype(o_ref.dtype)


def _window_attn(sink, q, k_ctx, v_ctx, k_lat, v_lat, *, tq):
    bsz, n_heads, n_tok, d = q.shape
    n_ctx = k_ctx.shape[2]
    kv_spec = lambda n: pl.BlockSpec((1, 1, n, d), lambda b, kvh, i: (b, kvh, 0, 0))
    return pl.pallas_call(
        _window_kernel,
        grid=(bsz, n_heads // 2, n_tok // tq),
        in_specs=[pl.BlockSpec(memory_space=pltpu.SMEM),
                  pl.BlockSpec((1, 2, tq, d), lambda b, kvh, i: (b, kvh, i, 0)),
                  kv_spec(n_ctx), kv_spec(n_ctx), kv_spec(n_tok), kv_spec(n_tok)],
        out_specs=pl.BlockSpec((1, tq, 2 * d), lambda b, kvh, i: (b, i, kvh)),
        out_shape=jax.ShapeDtypeStruct((bsz, n_tok, n_heads * d), BF16),
        compiler_params=_cparams(("parallel", "parallel", "arbitrary")),
        name="window_gqa",
    )(sink, q, k_ctx, v_ctx, k_lat, v_lat)


def _ctx_attn_kernel(sink_ref, qa_ref, ka_ref, va_ref, qb_ref, kb_ref, vb_ref, qc_ref, kc_ref, vc_ref,
                     ya_ref, yb_ref, yc_ref):
    n = qa_ref.shape[2]
    row1 = lax.broadcasted_iota(jnp.int32, (2 * n, 1), 0)

    def gqa(q_ref, k_ref, v_ref, y_ref, with_sink):
        outs = []
        for kvh in range(k_ref.shape[1]):
            q = q_ref[0, 2 * kvh:2 * kvh + 2].reshape(2 * n, q_ref.shape[3])
            sink = (jnp.where(row1 < n, sink_ref[2 * kvh], sink_ref[2 * kvh + 1]) * LOG2E
                    if with_sink else None)
            o = _softmax_attend(q, k_ref[0, kvh], v_ref[0, kvh, :, 0:HEAD_DIM], sink)
            outs += [o[:n], o[n:]]
        y_ref[0] = jnp.concatenate(outs, axis=-1).astype(y_ref.dtype)

    gqa(qa_ref, ka_ref, va_ref, ya_ref, True)
    gqa(qb_ref, kb_ref, vb_ref, yb_ref, False)
    outs = [_softmax_attend(qc_ref[0, hh], kc_ref[0, hh], vc_ref[0, hh, :, 0:C_V_DIM]) for hh in range(C_HEADS)]
    yc_ref[0] = jnp.concatenate(outs, axis=-1).astype(yc_ref.dtype)


def _ctx_attn(sink, cx):
    bsz, _, n, _ = cx[0].shape
    spec = lambda a: pl.BlockSpec((1, a.shape[1], n, a.shape[3]), lambda b: (b, 0, a.shape[2] // n - 1, 0))
    widths = (A_HEADS * HEAD_DIM, B_HEADS * HEAD_DIM, C_HEADS * C_V_DIM)
    return pl.pallas_call(
        _ctx_attn_kernel,
        grid=(bsz,),
        in_specs=[pl.BlockSpec(memory_space=pltpu.SMEM)] + [spec(a) for a in cx],
        out_specs=[pl.BlockSpec((1, n, w), lambda b: (b, 0, 0)) for w in widths],
        out_shape=[jax.ShapeDtypeStruct((bsz, n, w), BF16) for w in widths],
        compiler_params=_cparams(("parallel",)),
        name="ctx_attn",
    )(sink, *cx)


def _layer_norm(v, g, b):
    mu = jnp.mean(v, axis=-1, keepdims=True)
    d = v - mu
    var = jnp.mean(d * d, axis=-1, keepdims=True)
    return d * lax.rsqrt(var + NORM_EPS) * g + b


def _mlp_kernel(x_ref, ya_ref, yb_ref, yc_ref, mod_ref, w_out_ref, ln1_ref, w1_ref, w2_ref, ln2_ref,
                o_ref, *, ff_chunk):
    x = x_ref[0]
    g1, sh2, sc2, g2 = (mod_ref[0, r:r + 1, :] for r in (2, 3, 4, 5))
    y = (jnp.dot(ya_ref[0], w_out_ref[0:256, :], preferred_element_type=F32)
         + jnp.dot(yb_ref[0], w_out_ref[256:512, :], preferred_element_type=F32)
         + jnp.dot(yc_ref[0], w_out_ref[512:1024, :], preferred_element_type=F32))
    x1 = _layer_norm(DEEPNORM_ALPHA * x + g1 * y, ln1_ref[0:1, :], ln1_ref[1:2, :])
    h = (x1 * (1.0 + sc2) + sh2).astype(BF16)
    acc = jnp.zeros(x.shape, F32)
    for c0 in range(0, D_FF, ff_chunk):
        a = jnp.maximum(jnp.dot(h, w1_ref[:, c0:c0 + ff_chunk], preferred_element_type=F32), 0.0)
        acc = acc + jnp.dot((a * a).astype(BF16), w2_ref[c0:c0 + ff_chunk, :], preferred_element_type=F32)
    o_ref[0] = _layer_norm(DEEPNORM_ALPHA * x1 + g2 * acc, ln2_ref[0:1, :], ln2_ref[1:2, :])


def _mlp(x, ya, yb, yc, mod, wl, *, tm):
    bsz, n_tok, _ = x.shape
    tok = lambda width: pl.BlockSpec((1, tm, width), lambda b, i: (b, i, 0))
    const = lambda shape: pl.BlockSpec(shape, lambda b, i: (0, 0), pipeline_mode=pl.Buffered(1))
    return pl.pallas_call(
        functools.partial(_mlp_kernel, ff_chunk=1024),
        grid=(bsz, n_tok // tm),
        in_specs=[tok(D_MODEL), tok(256), tok(256), tok(512),
                  pl.BlockSpec((1, 6, D_MODEL), lambda b, i: (b, 0, 0)),
                  const((D_MODEL, D_MODEL)), const((2, D_MODEL)),
                  const((D_MODEL, D_FF)), const((D_FF, D_MODEL)), const((2, D_MODEL))],
        out_specs=tok(D_MODEL),
        out_shape=jax.ShapeDtypeStruct(x.shape, F32),
        compiler_params=_cparams(("parallel", "parallel")),
        name="outproj_mlp",
    )(x, ya, yb, yc, mod, wl["w_out"], wl["ln1"], wl["w_fc1"], wl["w_fc2"], wl["ln2"])


def _rope_tables(n_tok):
    def angles(rot_dim):
        t = jnp.arange(n_tok, dtype=jnp.int32)
        row, col = (t // GRID_W).astype(F32), (t % GRID_W).astype(F32)
        n_freq = rot_dim // 4
        inv = ROPE_THETA ** (-jnp.arange(n_freq, dtype=F32) / n_freq)
        ang = jnp.concatenate([row[:, None] * inv, col[:, None] * inv], axis=-1)
        return jnp.cos(ang), jnp.sin(ang)

    c64, s64 = angles(HEAD_DIM)
    c32, s32 = angles(C_ROPE_DIM)
    ones = lambda w: jnp.ones((n_tok, w), F32)
    zeros = lambda w: jnp.zeros((n_tok, w), F32)
    cos_a = jnp.tile(jnp.concatenate([c64, c64], axis=-1), (1, A_HEADS))
    sin_a = jnp.tile(jnp.concatenate([-s64, s64], axis=-1), (1, A_HEADS))
    cos_q = jnp.concatenate([ones(C_NOPE_DIM), c32, c32, ones(32)], axis=-1)
    sin_q = jnp.concatenate([zeros(C_NOPE_DIM), -s32, s32, zeros(32)], axis=-1)
    cos_k = jnp.concatenate([c32, c32, ones(96)], axis=-1)
    sin_k = jnp.concatenate([-s32, s32, zeros(96)], axis=-1)
    return cos_a, sin_a, cos_q, sin_q, cos_k, sin_k


def _layer_weights(l, w_in, q_norm_b, k_norm_b, mla_q_norm, mla_kv_norm, w_uq, w_uk, w_uv, w_out,
                   ln1_g, ln1_b, w_fc1, w_fc2, ln2_g, ln2_b):
    uq = w_uq[l].reshape(C_Q_RANK, C_HEADS, C_NOPE_DIM + C_ROPE_DIM)
    uq = jnp.pad(uq, ((0, 0), (0, 0), (0, C_QK_PAD - C_NOPE_DIM - C_ROPE_DIM)))
    uk = w_uk[l].reshape(C_KV_RANK, C_HEADS, C_NOPE_DIM)
    uk = jnp.pad(uk, ((0, 0), (0, 0), (0, C_QK_PAD - C_NOPE_DIM)))
    uv = w_uv[l].reshape(C_KV_RANK, C_HEADS, C_V_DIM)
    uv = jnp.pad(uv, ((0, 0), (0, 0), (0, V_PAD - C_V_DIM)))
    lane = jnp.arange(256)
    gmat = jnp.where((lane[:, None] // HEAD_DIM) == (lane[None, :] // HEAD_DIM), 1.0 / HEAD_DIM, 0.0)
    return {
        "w_in": jnp.pad(w_in[l], ((0, 0), (0, D_IN_PAD - D_IN))).astype(BF16),
        "w_uq": uq.reshape(C_Q_RANK, C_HEADS * C_QK_PAD).astype(BF16),
        "w_uk": uk.reshape(C_KV_RANK, C_HEADS * C_QK_PAD).astype(BF16),
        "w_uv": uv.reshape(C_KV_RANK, C_HEADS * V_PAD).astype(BF16),
        "gq": jnp.tile(q_norm_b[l], B_HEADS)[None, :],
        "gk": jnp.tile(k_norm_b[l], B_KV_HEADS)[None, :],
        "gmat": gmat.astype(BF16),
        "mqn": mla_q_norm[l][None, :],
        "mkvn": mla_kv_norm[l][None, :],
        "w_out": w_out[l].astype(BF16),
        "ln1": jnp.stack([ln1_g[l], ln1_b[l]]),
        "w_fc1": w_fc1[l].astype(BF16),
        "w_fc2": w_fc2[l].astype(BF16),
        "ln2": jnp.stack([ln2_g[l], ln2_b[l]]),
    }


def kernel(x, c, ctx, c_ctx, w_mod, b_mod, w_in, sink_a, q_norm_b, k_norm_b, mla_q_norm, mla_kv_norm,
           w_uq, w_uk, w_uv, w_out, ln1_g, ln1_b, w_fc1, w_fc2, ln2_g, ln2_b):
    bsz, n_tok, _ = x.shape
    n_ctx = ctx.shape[1]
    cc = jnp.concatenate([c, c_ctx[None, :], jnp.zeros((8 - bsz - 1, D_MODEL), F32)], axis=0)
    mods = _modulation(cc, w_mod, b_mod)
    tabs = _rope_tables(n_tok)
    tabs_ctx = tuple(t[:n_ctx] for t in tabs)
    xc = ctx
    for l in range(DEPTH):
        last = l == DEPTH - 1
        wl = _layer_weights(l, w_in, q_norm_b, k_norm_b, mla_q_norm, mla_kv_norm, w_uq, w_uk, w_uv, w_out,
                            ln1_g, ln1_b, w_fc1, w_fc2, ln2_g, ln2_b)
        mod_lat = mods[l, 0:bsz].reshape(bsz, 6, D_MODEL)
        mod_ctx = jnp.broadcast_to(mods[l, bsz].reshape(1, 6, D_MODEL), (bsz, 6, D_MODEL))
        n_all = n_tok + n_ctx
        lat = _inproj(x, mod_lat, wl, tabs, rope=True, tm=512, n_all=n_all)
        qa, ka, va, qb, qc = lat[0], lat[1], lat[2], lat[3], lat[6]
        cx = _inproj(xc, mod_ctx, wl, tabs_ctx, rope=False, tm=n_ctx, n_all=n_all,
                     merged=tuple(lat[o] for o in MERGED_OUTS))
        kb, vb, kc, vc = (cx[o] for o in MERGED_OUTS)
        ya = _window_attn(sink_a[l], qa, cx[1], cx[2], ka, va, tq=256)
        yb = _flash(qb, kb, vb, shared_kv=True, tq=1024, tk=768)
        yc = _flash(qc, kc, vc, shared_kv=False, tq=1024, tk=768)
        x = _mlp(x, ya, yb, yc, mod_lat, wl, tm=512)
        if not last:
            yca, ycb, ycc = _ctx_attn(sink_a[l], cx)
            xc = _mlp(xc, yca, ycb, ycc, mod_ctx, wl, tm=n_ctx)
    return x
```

```python
import functools

import jax
import jax.numpy as jnp
from jax import lax
from jax.experimental import pallas as pl
from jax.experimental.pallas import tpu as pltpu

F32 = jnp.float32
BF16 = jnp.bfloat16

D_MODEL = 1024
DEPTH = 2
GRID_W = 64
HEAD_DIM = 64
WINDOW = 128
A_HEADS, A_KV_HEADS = 4, 2
B_HEADS, B_KV_HEADS = 4, 2
C_HEADS = 8
C_Q_RANK, C_KV_RANK = 256, 128
C_NOPE_DIM, C_ROPE_DIM, C_V_DIM = 64, 32, 64
D_FF = 4 * D_MODEL
ROPE_THETA = 10000.0
NORM_EPS = 1e-6
NEG_INF = -1e30
DEEPNORM_ALPHA = (2 * DEPTH) ** 0.25

OFF_AQ, OFF_AK, OFF_AV = 0, 256, 384
OFF_BQ, OFF_BK, OFF_BV = 512, 768, 896
OFF_CQ, OFF_CKV, OFF_CKR = 1024, 1280, 1408
D_IN = 1440
D_IN_PAD = 1536
LANES = 128
C_QK_PAD = 128
V_PAD = 128
LOG2E = 1.4426950408889634

VMEM_LIMIT = 56 * 1024 * 1024


def _cparams(sem):
    return pltpu.CompilerParams(dimension_semantics=sem, vmem_limit_bytes=VMEM_LIMIT)


def _full(shape):
    n = len(shape)
    return pl.BlockSpec(shape, lambda *_: (0,) * n)


def _mod_kernel(cc_ref, w_ref, b_ref, o_ref):
    cc = cc_ref[...]
    s = cc / (1.0 + jnp.exp(-cc))
    o_ref[0] = jnp.dot(s, w_ref[0], preferred_element_type=F32,
                       precision=lax.Precision.HIGHEST) + b_ref[0]


def _modulation(cc, w_mod, b_mod):
    n_l = w_mod.shape[0]
    tn = 1024
    return pl.pallas_call(
        _mod_kernel,
        grid=(n_l, 6 * D_MODEL // tn),
        in_specs=[pl.BlockSpec((8, D_MODEL), lambda l, j: (0, 0)),
                  pl.BlockSpec((1, D_MODEL, tn), lambda l, j: (l, 0, j)),
                  pl.BlockSpec((1, 1, tn), lambda l, j: (l, 0, j))],
        out_specs=pl.BlockSpec((1, 8, tn), lambda l, j: (l, 0, j)),
        out_shape=jax.ShapeDtypeStruct((n_l, 8, 6 * D_MODEL), F32),
        compiler_params=_cparams(("parallel", "parallel")),
        name="modulation",
    )(cc, w_mod, b_mod.reshape(n_l, 1, 6 * D_MODEL))


def _rope(x, cos, sin_signed, half):
    width = x.shape[-1]
    lane = lax.broadcasted_iota(jnp.int32, x.shape, 1)
    first = (lane % (2 * half)) < half
    nxt = pltpu.roll(x, width - half, axis=1)
    prv = pltpu.roll(x, half, axis=1)
    return x * cos + jnp.where(first, nxt, prv) * sin_signed


def _group_mean_sq(x, g_ref):
    x2 = x * x
    hi = x2.astype(BF16)
    lo = (x2 - hi.astype(F32)).astype(BF16)
    g = g_ref[...]
    return (jnp.dot(hi, g, preferred_element_type=F32) + jnp.dot(lo, g, preferred_element_type=F32))


def _row_rms(x, gain):
    ms = jnp.mean(x * x, axis=-1, keepdims=True)
    return x * lax.rsqrt(ms + NORM_EPS) * gain


N_INPROJ_IN = 17
MERGED_OUTS = (4, 5, 7, 8)


def _inproj_kernel(*refs, rope):
    (x_ref, mod_ref, w_in_ref, w_uq_ref, w_uk_ref, w_uv_ref, gq_ref, gk_ref, gmat_ref, mqn_ref, mkvn_ref,
     cos_a_ref, sin_a_ref, cos_q_ref, sin_q_ref, cos_k_ref, sin_k_ref) = refs[:N_INPROJ_IN]
    qa_ref, ka_ref, va_ref, qb_ref, kb_ref, vb_ref, qc_ref, kc_ref, vc_ref = refs[-9:]
    x = x_ref[0]
    shift, scale = mod_ref[0, 0:1, :], mod_ref[0, 1:2, :]
    h = (x * (1.0 + scale) + shift).astype(BF16)
    p = jnp.dot(h, w_in_ref[...], preferred_element_type=F32)

    def store_heads(ref, val, n_heads, width):
        for hh in range(n_heads):
            ref[0, hh] = val[:, hh * width:(hh + 1) * width].astype(ref.dtype)

    v_lanes = lax.broadcasted_iota(jnp.int32, (x.shape[0], V_PAD), 1) < HEAD_DIM

    def store_values(ref, val):
        ref[0, 0] = jnp.where(v_lanes, val, 1.0).astype(BF16)
        ref[0, 1] = jnp.where(v_lanes, pltpu.roll(val, HEAD_DIM, axis=1), 1.0).astype(BF16)

    qa = p[:, OFF_AQ:OFF_AQ + 256]
    ka = p[:, OFF_AK:OFF_AK + 128]
    if rope:
        qa = _rope(qa, cos_a_ref[...], sin_a_ref[...], HEAD_DIM // 2)
        ka = _rope(ka, cos_a_ref[:, 0:128], sin_a_ref[:, 0:128], HEAD_DIM // 2)
    store_heads(qa_ref, qa * (HEAD_DIM ** -0.5 * LOG2E), A_HEADS, HEAD_DIM)
    store_heads(ka_ref, ka, A_KV_HEADS, HEAD_DIM)
    store_values(va_ref, p[:, OFF_AV:OFF_AV + 128])

    qb = p[:, OFF_BQ:OFF_BQ + 256]
    kb = p[:, OFF_BK:OFF_BK + 128]
    qb = qb * lax.rsqrt(_group_mean_sq(qb, gmat_ref) + NORM_EPS) * gq_ref[...]
    kb = kb * lax.rsqrt(_group_mean_sq(kb, gmat_ref.at[0:128, 0:128]) + NORM_EPS) * gk_ref[...]
    if rope:
        qb = _rope(qb, cos_a_ref[...], sin_a_ref[...], HEAD_DIM // 2)
        kb = _rope(kb, cos_a_ref[:, 0:128], sin_a_ref[:, 0:128], HEAD_DIM // 2)
    store_heads(qb_ref, qb * (HEAD_DIM ** -0.5 * LOG2E), B_HEADS, HEAD_DIM)
    store_heads(kb_ref, kb, B_KV_HEADS, HEAD_DIM)
    store_values(vb_ref, p[:, OFF_BV:OFF_BV + 128])

    cq = _row_rms(p[:, OFF_CQ:OFF_CQ + C_Q_RANK], mqn_ref[...]).astype(BF16)
    ckv = _row_rms(p[:, OFF_CKV:OFF_CKV + C_KV_RANK], mkvn_ref[...]).astype(BF16)
    q = jnp.dot(cq, w_uq_ref[...], preferred_element_type=F32)
    kn = jnp.dot(ckv, w_uk_ref[...], preferred_element_type=F32)
    vc = jnp.dot(ckv, w_uv_ref[...], preferred_element_type=F32)
    kr = p[:, OFF_CKR:OFF_CKR + LANES]
    if rope:
        kr = _rope(kr, cos_k_ref[...], sin_k_ref[...], C_ROPE_DIM // 2)
    kr = pltpu.roll(kr, C_NOPE_DIM, axis=1)
    c_scale = (C_NOPE_DIM + C_ROPE_DIM) ** -0.5 * LOG2E
    for hh in range(C_HEADS):
        qh = q[:, hh * C_QK_PAD:(hh + 1) * C_QK_PAD]
        if rope:
            qh = _rope(qh, cos_q_ref[...], sin_q_ref[...], C_ROPE_DIM // 2)
        qc_ref[0, hh] = (qh * c_scale).astype(BF16)
        kc_ref[0, hh] = (kn[:, hh * C_QK_PAD:(hh + 1) * C_QK_PAD] + kr).astype(BF16)
        vc_ref[0, hh] = jnp.where(v_lanes, vc[:, hh * V_PAD:(hh + 1) * V_PAD], 1.0).astype(BF16)


def _inproj(x, mod, wl, tabs, *, rope, tm, n_all, merged=()):
    bsz, n_tok, _ = x.shape
    nt = n_tok // tm
    row_off = (n_all - n_tok) // tm if merged else 0
    out_dims = ((A_HEADS, HEAD_DIM), (A_KV_HEADS, HEAD_DIM), (A_KV_HEADS, V_PAD),
                (B_HEADS, HEAD_DIM), (B_KV_HEADS, HEAD_DIM), (B_KV_HEADS, V_PAD),
                (C_HEADS, C_QK_PAD), (C_HEADS, C_QK_PAD), (C_HEADS, V_PAD))
    tok = lambda width: pl.BlockSpec((tm, width), lambda i, b: (i, 0))

    def out_spec(o, n, width):
        off = row_off if o in MERGED_OUTS else 0
        return pl.BlockSpec((1, n, tm, width), lambda i, b: (b, 0, i + off, 0))

    def out_shape(o, n, width):
        return jax.ShapeDtypeStruct((bsz, n, n_all if o in MERGED_OUTS else n_tok, width), BF16)

    return pl.pallas_call(
        functools.partial(_inproj_kernel, rope=rope),
        grid=(nt, bsz),
        in_specs=[pl.BlockSpec((1, tm, D_MODEL), lambda i, b: (b, i, 0)),
                  pl.BlockSpec((1, 6, D_MODEL), lambda i, b: (b, 0, 0)),
                  _full((D_MODEL, D_IN_PAD)), _full((C_Q_RANK, C_HEADS * C_QK_PAD)),
                  _full((C_KV_RANK, C_HEADS * C_QK_PAD)), _full((C_KV_RANK, C_HEADS * V_PAD)),
                  _full((1, 256)), _full((1, 128)), _full((256, 256)),
                  _full((1, C_Q_RANK)), _full((1, C_KV_RANK)),
                  tok(256), tok(256), tok(128), tok(128), tok(128), tok(128)]
                 + [pl.BlockSpec(memory_space=pl.ANY)] * len(merged),
        out_specs=[out_spec(o, *d) for o, d in enumerate(out_dims)],
        out_shape=[out_shape(o, *d) for o, d in enumerate(out_dims)],
        input_output_aliases={N_INPROJ_IN + j: o for j, o in enumerate(MERGED_OUTS)} if merged else {},
        compiler_params=_cparams(("parallel", "parallel")),
        name="inproj_rope" if rope else "inproj_ctx",
    )(x, mod, wl["w_in"], wl["w_uq"], wl["w_uk"], wl["w_uv"], wl["gq"], wl["gk"], wl["gmat"],
      wl["mqn"], wl["mkvn"], *tabs, *merged)


def _qk(q, k):
    return lax.dot_general(q, k, (((1,), (1,)), ((), ())), preferred_element_type=F32)


def _online_step(q, k, v, m, acc):
    s = _qk(q, k)
    m_new = jnp.maximum(m, jnp.max(s, axis=-1, keepdims=True))
    p = jnp.exp2(s - m_new)
    acc = jnp.exp2(m - m_new) * acc + jnp.dot(p.astype(v.dtype), v, preferred_element_type=F32)
    return m_new, acc


def _softmax_attend(q, k, v, sink=None):
    s = _qk(q, k)
    m = jnp.max(s, axis=-1, keepdims=True)
    if sink is not None:
        m = jnp.maximum(m, sink)
    p = jnp.exp2(s - m)
    l = jnp.sum(p, axis=-1, keepdims=True)
    if sink is not None:
        l = l + jnp.exp2(sink - m)
    return jnp.dot(p.astype(v.dtype), v, preferred_element_type=F32) / l


def _flash_kernel(q_ref, k_ref, v_ref, o_ref, *, shared_kv, tk):
    tq = q_ref.shape[2]
    dv = o_ref.shape[2] // 2
    heads = ((q_ref[0, 0], 0), (q_ref[0, 1], 0 if shared_kv else 1))
    carry = [(jnp.full((tq, 1), NEG_INF, F32), jnp.zeros((tq, v_ref.shape[3]), F32)) for _ in heads]
    for off in range(0, k_ref.shape[2], tk):
        carry = [_online_step(q, k_ref[0, hk, off:off + tk, :], v_ref[0, hk, off:off + tk, :], *c)
                 for (q, hk), c in zip(heads, carry)]
    outs = [(acc / pltpu.roll(acc, dv, axis=1))[:, :dv] for _, acc in carry]
    o_ref[0] = jnp.concatenate(outs, axis=-1).astype(o_ref.dtype)


def _flash(q, k, v, *, shared_kv, tq, tk):
    bsz, n_heads, n_tok, dk = q.shape
    n_keys, dvp = v.shape[2], v.shape[3]
    dv = dvp // 2
    kvb = 1 if shared_kv else 2
    kv_spec = lambda width: pl.BlockSpec((1, kvb, n_keys, width), lambda b, hp, i: (b, hp, 0, 0))
    return pl.pallas_call(
        functools.partial(_flash_kernel, shared_kv=shared_kv, tk=tk),
        grid=(bsz, n_heads // 2, n_tok // tq),
        in_specs=[pl.BlockSpec((1, 2, tq, dk), lambda b, hp, i: (b, hp, i, 0)), kv_spec(dk), kv_spec(dvp)],
        out_specs=pl.BlockSpec((1, tq, 2 * dv), lambda b, hp, i: (b, i, hp)),
        out_shape=jax.ShapeDtypeStruct((bsz, n_tok, n_heads * dv), BF16),
        compiler_params=_cparams(("parallel", "parallel", "arbitrary")),
        name="flash_gqa" if shared_kv else "flash_mla",
    )(q, k, v)


def _window_kernel(sink_ref, q_ref, bias_ref, kc_ref, vc_ref, kl_ref, vl_ref, o_ref):
    tq, d = q_ref.shape[2], q_ref.shape[3]
    n_tok, span = kl_ref.shape[2], bias_ref.shape[2]
    start = pl.multiple_of(jnp.clip(pl.program_id(1) * tq - WINDOW, 0, n_tok - span), WINDOW)
    bias = bias_ref[...]
    row1 = lax.broadcasted_iota(jnp.int32, (2 * tq, 1), 0)
    outs = []
    for kvh in range(kl_ref.shape[1]):
        q = q_ref[0, 2 * kvh:2 * kvh + 2].reshape(2 * tq, d)
        s_loc = (_qk(q, kl_ref[0, kvh, pl.ds(start, span), :]).reshape(2, tq, span) + bias).reshape(2 * tq, span)
        s_ctx = _qk(q, kc_ref[0, kvh])
        sink = jnp.where(row1 < tq, sink_ref[2 * kvh], sink_ref[2 * kvh + 1]) * LOG2E
        m = jnp.maximum(jnp.maximum(jnp.max(s_loc, axis=-1, keepdims=True),
                                    jnp.max(s_ctx, axis=-1, keepdims=True)), sink)
        acc = (jnp.dot(jnp.exp2(s_loc - m).astype(BF16), vl_ref[0, kvh, pl.ds(start, span), :],
                       preferred_element_type=F32)
               + jnp.dot(jnp.exp2(s_ctx - m).astype(BF16), vc_ref[0, kvh], preferred_element_type=F32))
        o = (acc / (pltpu.roll(acc, d, axis=1) + jnp.exp2(sink - m)))[:, :d]
        outs += [o[:tq], o[tq:]]
    o_ref[0] = jnp.concatenate(outs, axis=-1).astype(o_ref.dtype)


def _window_attn(sink, q, k_ctx, v_ctx, k_lat, v_lat, *, tq):
    bsz, n_heads, n_tok, d = q.shape
    nt = n_tok // tq
    span = tq + 2 * WINDOW
    rel = jnp.arange(span)[None, None, :] - jnp.arange(tq)[None, :, None] + jnp.array([0, -WINDOW, tq - span])[:, None, None]
    bias = jnp.where(jnp.abs(rel) <= WINDOW, 0.0, NEG_INF).astype(F32)
    kv_spec = lambda a: pl.BlockSpec((1,) + a.shape[1:], lambda b, i: (b, 0, 0, 0))
    return pl.pallas_call(
        _window_kernel,
        grid=(bsz, nt),
        in_specs=[pl.BlockSpec(memory_space=pltpu.SMEM),
                  pl.BlockSpec((1, n_heads, tq, d), lambda b, i: (b, 0, i, 0)),
                  pl.BlockSpec((1, tq, span), lambda b, i: (jnp.where(i == 0, 0, jnp.where(i == nt - 1, 2, 1)), 0, 0)),
                  kv_spec(k_ctx), kv_spec(v_ctx), kv_spec(k_lat), kv_spec(v_lat)],
        out_specs=pl.BlockSpec((1, tq, n_heads * d), lambda b, i: (b, i, 0)),
        out_shape=jax.ShapeDtypeStruct((bsz, n_tok, n_heads * d), BF16),
        compiler_params=_cparams(("parallel", "arbitrary")),
        name="window_gqa",
    )(sink, q, bias, k_ctx, v_ctx, k_lat, v_lat)


def _ctx_attn_kernel(sink_ref, qa_ref, ka_ref, va_ref, qb_ref, kb_ref, vb_ref, qc_ref, kc_ref, vc_ref,
                     ya_ref, yb_ref, yc_ref):
    n = qa_ref.shape[2]
    row1 = lax.broadcasted_iota(jnp.int32, (2 * n, 1), 0)

    def gqa(q_ref, k_ref, v_ref, y_ref, with_sink):
        outs = []
        for kvh in range(k_ref.shape[1]):
            q = q_ref[0, 2 * kvh:2 * kvh + 2].reshape(2 * n, q_ref.shape[3])
            sink = (jnp.where(row1 < n, sink_ref[2 * kvh], sink_ref[2 * kvh + 1]) * LOG2E
                    if with_sink else None)
            o = _softmax_attend(q, k_ref[0, kvh], v_ref[0, kvh, :, 0:HEAD_DIM], sink)
            outs += [o[:n], o[n:]]
        y_ref[0] = jnp.concatenate(outs, axis=-1).astype(y_ref.dtype)

    gqa(qa_ref, ka_ref, va_ref, ya_ref, True)
    gqa(qb_ref, kb_ref, vb_ref, yb_ref, False)
    outs = [_softmax_attend(qc_ref[0, hh], kc_ref[0, hh], vc_ref[0, hh, :, 0:C_V_DIM]) for hh in range(C_HEADS)]
    yc_ref[0] = jnp.concatenate(outs, axis=-1).astype(yc_ref.dtype)


def _ctx_attn(sink, cx):
    bsz, _, n, _ = cx[0].shape
    spec = lambda a: pl.BlockSpec((1, a.shape[1], n, a.shape[3]), lambda b: (b, 0, a.shape[2] // n - 1, 0))
    widths = (A_HEADS * HEAD_DIM, B_HEADS * HEAD_DIM, C_HEADS * C_V_DIM)
    return pl.pallas_call(
        _ctx_attn_kernel,
        grid=(bsz,),
        in_specs=[pl.BlockSpec(memory_space=pltpu.SMEM)] + [spec(a) for a in cx],
        out_specs=[pl.BlockSpec((1, n, w), lambda b: (b, 0, 0)) for w in widths],
        out_shape=[jax.ShapeDtypeStruct((bsz, n, w), BF16) for w in widths],
        compiler_params=_cparams(("parallel",)),
        name="ctx_attn",
    )(sink, *cx)


def _layer_norm(v, g, b):
    mu = jnp.mean(v, axis=-1, keepdims=True)
    d = v - mu
    var = jnp.mean(d * d, axis=-1, keepdims=True)
    return d * lax.rsqrt(var + NORM_EPS) * g + b


def _mlp_kernel(x_ref, ya_ref, yb_ref, yc_ref, mod_ref, w_out_ref, ln1_ref, w1_ref, w2_ref, ln2_ref,
                o_ref, *, ff_chunk):
    x = x_ref[0]
    g1, sh2, sc2, g2 = (mod_ref[0, r:r + 1, :] for r in (2, 3, 4, 5))
    y = (jnp.dot(ya_ref[0], w_out_ref[0:256, :], preferred_element_type=F32)
         + jnp.dot(yb_ref[0], w_out_ref[256:512, :], preferred_element_type=F32)
         + jnp.dot(yc_ref[0], w_out_ref[512:1024, :], preferred_element_type=F32))
    x1 = _layer_norm(DEEPNORM_ALPHA * x + g1 * y, ln1_ref[0:1, :], ln1_ref[1:2, :])
    h = (x1 * (1.0 + sc2) + sh2).astype(BF16)
    acc = jnp.zeros(x.shape, F32)
    for c0 in range(0, D_FF, ff_chunk):
        a = jnp.maximum(jnp.dot(h, w1_ref[:, c0:c0 + ff_chunk], preferred_element_type=F32), 0.0)
        acc = acc + jnp.dot((a * a).astype(BF16), w2_ref[c0:c0 + ff_chunk, :], preferred_element_type=F32)
    o_ref[0] = _layer_norm(DEEPNORM_ALPHA * x1 + g2 * acc, ln2_ref[0:1, :], ln2_ref[1:2, :])


def _mlp(x, ya, yb, yc, mod, wl, *, tm):
    bsz, n_tok, _ = x.shape
    tok = lambda width: pl.BlockSpec((1, tm, width), lambda b, i: (b, i, 0))
    const = lambda shape: pl.BlockSpec(shape, lambda b, i: (0, 0), pipeline_mode=pl.Buffered(1))
    return pl.pallas_call(
        functools.partial(_mlp_kernel, ff_chunk=1024),
        grid=(bsz, n_tok // tm),
        in_specs=[tok(D_MODEL), tok(256), tok(256), tok(512),
                  pl.BlockSpec((1, 6, D_MODEL), lambda b, i: (b, 0, 0)),
                  const((D_MODEL, D_MODEL)), const((2, D_MODEL)),
                  const((D_MODEL, D_FF)), const((D_FF, D_MODEL)), const((2, D_MODEL))],
        out_specs=tok(D_MODEL),
        out_shape=jax.ShapeDtypeStruct(x.shape, F32),
        compiler_params=_cparams(("parallel", "parallel")),
        name="outproj_mlp",
    )(x, ya, yb, yc, mod, wl["w_out"], wl["ln1"], wl["w_fc1"], wl["w_fc2"], wl["ln2"])


def _rope_tables(n_tok):
    def angles(rot_dim):
        t = jnp.arange(n_tok, dtype=jnp.int32)
        row, col = (t // GRID_W).astype(F32), (t % GRID_W).astype(F32)
        n_freq = rot_dim // 4
        inv = ROPE_THETA ** (-jnp.arange(n_freq, dtype=F32) / n_freq)
        ang = jnp.concatenate([row[:, None] * inv, col[:, None] * inv], axis=-1)
        return jnp.cos(ang), jnp.sin(ang)

    c64, s64 = angles(HEAD_DIM)
    c32, s32 = angles(C_ROPE_DIM)
    ones = lambda w: jnp.ones((n_tok, w), F32)
    zeros = lambda w: jnp.zeros((n_tok, w), F32)
    cos_a = jnp.tile(jnp.concatenate([c64, c64], axis=-1), (1, A_HEADS))
    sin_a = jnp.tile(jnp.concatenate([-s64, s64], axis=-1), (1, A_HEADS))
    cos_q = jnp.concatenate([ones(C_NOPE_DIM), c32, c32, ones(32)], axis=-1)
    sin_q = jnp.concatenate([zeros(C_NOPE_DIM), -s32, s32, zeros(32)], axis=-1)
    cos_k = jnp.concatenate([c32, c32, ones(96)], axis=-1)
    sin_k = jnp.concatenate([-s32, s32, zeros(96)], axis=-1)
    return cos_a, sin_a, cos_q, sin_q, cos_k, sin_k


def _layer_weights(l, w_in, q_norm_b, k_norm_b, mla_q_norm, mla_kv_norm, w_uq, w_uk, w_uv, w_out,
                   ln1_g, ln1_b, w_fc1, w_fc2, ln2_g, ln2_b):
    uq = w_uq[l].reshape(C_Q_RANK, C_HEADS, C_NOPE_DIM + C_ROPE_DIM)
    uq = jnp.pad(uq, ((0, 0), (0, 0), (0, C_QK_PAD - C_NOPE_DIM - C_ROPE_DIM)))
    uk = w_uk[l].reshape(C_KV_RANK, C_HEADS, C_NOPE_DIM)
    uk = jnp.pad(uk, ((0, 0), (0, 0), (0, C_QK_PAD - C_NOPE_DIM)))
    uv = w_uv[l].reshape(C_KV_RANK, C_HEADS, C_V_DIM)
    uv = jnp.pad(uv, ((0, 0), (0, 0), (0, V_PAD - C_V_DIM)))
    lane = jnp.arange(256)
    gmat = jnp.where((lane[:, None] // HEAD_DIM) == (lane[None, :] // HEAD_DIM), 1.0 / HEAD_DIM, 0.0)
    return {
        "w_in": jnp.pad(w_in[l], ((0, 0), (0, D_IN_PAD - D_IN))).astype(BF16),
        "w_uq": uq.reshape(C_Q_RANK, C_HEADS * C_QK_PAD).astype(BF16),
        "w_uk": uk.reshape(C_KV_RANK, C_HEADS * C_QK_PAD).astype(BF16),
        "w_uv": uv.reshape(C_KV_RANK, C_HEADS * V_PAD).astype(BF16),
        "gq": jnp.tile(q_norm_b[l], B_HEADS)[None, :],
        "gk": jnp.tile(k_norm_b[l], B_KV_HEADS)[None, :],
        "gmat": gmat.astype(BF16),
        "mqn": mla_q_norm[l][None, :],
        "mkvn": mla_kv_norm[l][None, :],
        "w_out": w_out[l].astype(BF16),
        "ln1": jnp.stack([ln1_g[l], ln1_b[l]]),
        "w_fc1": w_fc1[l].astype(BF16),
        "w_fc2": w_fc2[l].astype(BF16),
        "ln2": jnp.stack([ln2_g[l], ln2_b[l]]),
    }


def kernel(x, c, ctx, c_ctx, w_mod, b_mod, w_in, sink_a, q_norm_b, k_norm_b, mla_q_norm, mla_kv_norm,
           w_uq, w_uk, w_uv, w_out, ln1_g, ln1_b, w_fc1, w_fc2, ln2_g, ln2_b):
    bsz, n_tok, _ = x.shape
    n_ctx = ctx.shape[1]
    cc = jnp.concatenate([c, c_ctx[None, :], jnp.zeros((8 - bsz - 1, D_MODEL), F32)], axis=0)
    mods = _modulation(cc, w_mod, b_mod)
    tabs = _rope_tables(n_tok)
    tabs_ctx = tuple(t[:n_ctx] for t in tabs)
    xc = ctx
    for l in range(DEPTH):
        last = l == DEPTH - 1
        wl = _layer_weights(l, w_in, q_norm_b, k_norm_b, mla_q_norm, mla_kv_norm, w_uq, w_uk, w_uv, w_out,
                            ln1_g, ln1_b, w_fc1, w_fc2, ln2_g, ln2_b)
        mod_lat = mods[l, 0:bsz].reshape(bsz, 6, D_MODEL)
        mod_ctx = jnp.broadcast_to(mods[l, bsz].reshape(1, 6, D_MODEL), (bsz, 6, D_MODEL))
        n_all = n_tok + n_ctx
        lat = _inproj(x, mod_lat, wl, tabs, rope=True, tm=512, n_all=n_all)
        qa, ka, va, qb, qc = lat[0], lat[1], lat[2], lat[3], lat[6]
        cx = _inproj(xc, mod_ctx, wl, tabs_ctx, rope=False, tm=n_ctx, n_all=n_all,
                     merged=tuple(lat[o] for o in MERGED_OUTS))
        kb, vb, kc, vc = (cx[o] for o in MERGED_OUTS)
        ya = _window_attn(sink_a[l], qa, cx[1], cx[2], ka, va, tq=256)
        yb = _flash(qb, kb, vb, shared_kv=True, tq=1024, tk=768)
        yc = _flash(qc, kc, vc, shared_kv=False, tq=1024, tk=768)
        x = _mlp(x, ya, yb, yc, mod_lat, wl, tm=512)
        if not last:
            yca, ycb, ycc = _ctx_attn(sink_a[l], cx)
            xc = _mlp(xc, yca, ycb, ycc, mod_ctx, wl, tm=n_ctx)
    return x
```

```python
import functools

import jax
import jax.numpy as jnp
from jax import lax
from jax.experimental import pallas as pl
from jax.experimental.pallas import tpu as pltpu

F32 = jnp.float32
BF16 = jnp.bfloat16

D_MODEL = 1024
DEPTH = 2
GRID_W = 64
HEAD_DIM = 64
WINDOW = 128
A_HEADS, A_KV_HEADS = 4, 2
B_HEADS, B_KV_HEADS = 4, 2
C_HEADS = 8
C_Q_RANK, C_KV_RANK = 256, 128
C_NOPE_DIM, C_ROPE_DIM, C_V_DIM = 64, 32, 64
D_FF = 4 * D_MODEL
ROPE_THETA = 10000.0
NORM_EPS = 1e-6
NEG_INF = -1e30
DEEPNORM_ALPHA = (2 * DEPTH) ** 0.25

OFF_AQ, OFF_AK, OFF_AV = 0, 256, 384
OFF_BQ, OFF_BK, OFF_BV = 512, 768, 896
OFF_CQ, OFF_CKV, OFF_CKR = 1024, 1280, 1408
D_IN = 1440
D_IN_PAD = 1536
LANES = 128
C_QK_PAD = 128
V_PAD = 128
LOG2E = 1.4426950408889634

VMEM_LIMIT = 56 * 1024 * 1024


def _cparams(sem):
    return pltpu.CompilerParams(dimension_semantics=sem, vmem_limit_bytes=VMEM_LIMIT)


def _full(shape):
    n = len(shape)
    return pl.BlockSpec(shape, lambda *_: (0,) * n)


def _mod_kernel(cc_ref, w_ref, b_ref, o_ref):
    cc = cc_ref[...]
    s = cc / (1.0 + jnp.exp(-cc))
    o_ref[0] = jnp.dot(s, w_ref[0], preferred_element_type=F32,
                       precision=lax.Precision.HIGHEST) + b_ref[0]


def _modulation(cc, w_mod, b_mod):
    n_l = w_mod.shape[0]
    tn = 1024
    return pl.pallas_call(
        _mod_kernel,
        grid=(n_l, 6 * D_MODEL // tn),
        in_specs=[pl.BlockSpec((8, D_MODEL), lambda l, j: (0, 0)),
                  pl.BlockSpec((1, D_MODEL, tn), lambda l, j: (l, 0, j)),
                  pl.BlockSpec((1, 1, tn), lambda l, j: (l, 0, j))],
        out_specs=pl.BlockSpec((1, 8, tn), lambda l, j: (l, 0, j)),
        out_shape=jax.ShapeDtypeStruct((n_l, 8, 6 * D_MODEL), F32),
        compiler_params=_cparams(("parallel", "parallel")),
        name="modulation",
    )(cc, w_mod, b_mod.reshape(n_l, 1, 6 * D_MODEL))


def _rope(x, cos, sin_signed, half):
    width = x.shape[-1]
    lane = lax.broadcasted_iota(jnp.int32, x.shape, 1)
    first = (lane % (2 * half)) < half
    nxt = pltpu.roll(x, width - half, axis=1)
    prv = pltpu.roll(x, half, axis=1)
    return x * cos + jnp.where(first, nxt, prv) * sin_signed


def _rope_halves(x, cos, sin_signed):
    return x * cos + pltpu.roll(x, LANES // 2, axis=1) * sin_signed


def _group_mean_sq(x, g_ref):
    x2 = x * x
    hi = x2.astype(BF16)
    lo = (x2 - hi.astype(F32)).astype(BF16)
    g = g_ref[...]
    return (jnp.dot(hi, g, preferred_element_type=F32) + jnp.dot(lo, g, preferred_element_type=F32))


def _row_rms(x, gain):
    ms = jnp.mean(x * x, axis=-1, keepdims=True)
    return x * lax.rsqrt(ms + NORM_EPS) * gain


N_INPROJ_IN = 15
MERGED_OUTS = (4, 5, 7, 8)


def _inproj_kernel(*refs, rope):
    (x_ref, mod_ref, w_in_ref, w_uq_ref, w_uk_ref, w_uv_ref, gq_ref, gk_ref, gmat_ref, mqn_ref, mkvn_ref,
     cos_a_ref, sin_a_ref, cos_q_ref, sin_q_ref) = refs[:N_INPROJ_IN]
    qa_ref, ka_ref, va_ref, qb_ref, kb_ref, vb_ref, qc_ref, kc_ref, vc_ref = refs[-9:]
    x = x_ref[0]
    shift, scale = mod_ref[0, 0:1, :], mod_ref[0, 1:2, :]
    h = (x * (1.0 + scale) + shift).astype(BF16)
    p = jnp.dot(h, w_in_ref[...], preferred_element_type=F32)

    def store_heads(ref, val, n_heads, width):
        for hh in range(n_heads):
            ref[0, hh] = val[:, hh * width:(hh + 1) * width].astype(ref.dtype)

    v_lanes = lax.broadcasted_iota(jnp.int32, (x.shape[0], V_PAD), 1) < HEAD_DIM

    def store_values(ref, val):
        ref[0, 0] = jnp.where(v_lanes, val, 1.0).astype(BF16)
        ref[0, 1] = jnp.where(v_lanes, pltpu.roll(val, HEAD_DIM, axis=1), 1.0).astype(BF16)

    qa = p[:, OFF_AQ:OFF_AQ + 256]
    ka = p[:, OFF_AK:OFF_AK + 128]
    if rope:
        qa = _rope(qa, cos_a_ref[...], sin_a_ref[...], HEAD_DIM // 2)
        ka = _rope(ka, cos_a_ref[:, 0:128], sin_a_ref[:, 0:128], HEAD_DIM // 2)
    store_heads(qa_ref, qa * (HEAD_DIM ** -0.5 * LOG2E), A_HEADS, HEAD_DIM)
    store_heads(ka_ref, ka, A_KV_HEADS, HEAD_DIM)
    store_values(va_ref, p[:, OFF_AV:OFF_AV + 128])

    qb = p[:, OFF_BQ:OFF_BQ + 256]
    kb = p[:, OFF_BK:OFF_BK + 128]
    qb = qb * lax.rsqrt(_group_mean_sq(qb, gmat_ref) + NORM_EPS) * gq_ref[...]
    kb = kb * lax.rsqrt(_group_mean_sq(kb, gmat_ref.at[0:128, 0:128]) + NORM_EPS) * gk_ref[...]
    if rope:
        qb = _rope(qb, cos_a_ref[...], sin_a_ref[...], HEAD_DIM // 2)
        kb = _rope(kb, cos_a_ref[:, 0:128], sin_a_ref[:, 0:128], HEAD_DIM // 2)
    store_heads(qb_ref, qb * (HEAD_DIM ** -0.5 * LOG2E), B_HEADS, HEAD_DIM)
    store_heads(kb_ref, kb, B_KV_HEADS, HEAD_DIM)
    store_values(vb_ref, p[:, OFF_BV:OFF_BV + 128])

    cq = _row_rms(p[:, OFF_CQ:OFF_CQ + C_Q_RANK], mqn_ref[...]).astype(BF16)
    ckv = _row_rms(p[:, OFF_CKV:OFF_CKV + C_KV_RANK], mkvn_ref[...]).astype(BF16)
    q = jnp.dot(cq, w_uq_ref[...], preferred_element_type=F32)
    kn = jnp.dot(ckv, w_uk_ref[...], preferred_element_type=F32)
    vc = jnp.dot(ckv, w_uv_ref[...], preferred_element_type=F32)
    kr = p[:, OFF_CKR:OFF_CKR + LANES]
    if rope:
        kr = _rope_halves(kr, cos_q_ref[...], sin_q_ref[...])
    c_scale = (C_NOPE_DIM + C_ROPE_DIM) ** -0.5 * LOG2E
    for hh in range(C_HEADS):
        qh = q[:, hh * C_QK_PAD:(hh + 1) * C_QK_PAD]
        if rope:
            qh = _rope_halves(qh, cos_q_ref[...], sin_q_ref[...])
        qc_ref[0, hh] = (qh * c_scale).astype(BF16)
        kc_ref[0, hh] = (kn[:, hh * C_QK_PAD:(hh + 1) * C_QK_PAD] + kr).astype(BF16)
        vc_ref[0, hh] = jnp.where(v_lanes, vc[:, hh * V_PAD:(hh + 1) * V_PAD], 1.0).astype(BF16)


def _inproj(x, mod, wl, tabs, *, rope, tm, n_all, merged=()):
    bsz, n_tok, _ = x.shape
    nt = n_tok // tm
    row_off = (n_all - n_tok) // tm if merged else 0
    out_dims = ((A_HEADS, HEAD_DIM), (A_KV_HEADS, HEAD_DIM), (A_KV_HEADS, V_PAD),
                (B_HEADS, HEAD_DIM), (B_KV_HEADS, HEAD_DIM), (B_KV_HEADS, V_PAD),
                (C_HEADS, C_QK_PAD), (C_HEADS, C_QK_PAD), (C_HEADS, V_PAD))
    tok = lambda width: pl.BlockSpec((tm, width), lambda i, b: (i, 0))

    def out_spec(o, n, width):
        off = row_off if o in MERGED_OUTS else 0
        return pl.BlockSpec((1, n, tm, width), lambda i, b: (b, 0, i + off, 0))

    def out_shape(o, n, width):
        return jax.ShapeDtypeStruct((bsz, n, n_all if o in MERGED_OUTS else n_tok, width), BF16)

    return pl.pallas_call(
        functools.partial(_inproj_kernel, rope=rope),
        grid=(nt, bsz),
        in_specs=[pl.BlockSpec((1, tm, D_MODEL), lambda i, b: (b, i, 0)),
                  pl.BlockSpec((1, 6, D_MODEL), lambda i, b: (b, 0, 0)),
                  _full((D_MODEL, D_IN_PAD)), _full((C_Q_RANK, C_HEADS * C_QK_PAD)),
                  _full((C_KV_RANK, C_HEADS * C_QK_PAD)), _full((C_KV_RANK, C_HEADS * V_PAD)),
                  _full((1, 256)), _full((1, 128)), _full((256, 256)),
                  _full((1, C_Q_RANK)), _full((1, C_KV_RANK)),
                  tok(256), tok(256), tok(128), tok(128)]
                 + [pl.BlockSpec(memory_space=pl.ANY)] * len(merged),
        out_specs=[out_spec(o, *d) for o, d in enumerate(out_dims)],
        out_shape=[out_shape(o, *d) for o, d in enumerate(out_dims)],
        input_output_aliases={N_INPROJ_IN + j: o for j, o in enumerate(MERGED_OUTS)} if merged else {},
        compiler_params=_cparams(("parallel", "parallel")),
        name="inproj_rope" if rope else "inproj_ctx",
    )(x, mod, wl["w_in"], wl["w_uq"], wl["w_uk"], wl["w_uv"], wl["gq"], wl["gk"], wl["gmat"],
      wl["mqn"], wl["mkvn"], *tabs, *merged)


def _qk(q, k):
    return lax.dot_general(q, k, (((1,), (1,)), ((), ())), preferred_element_type=F32)


def _online_step(q, k, v, m, acc):
    s = _qk(q, k)
    m_new = jnp.maximum(m, jnp.max(s, axis=-1, keepdims=True))
    p = jnp.exp2(s - m_new)
    acc = jnp.exp2(m - m_new) * acc + jnp.dot(p.astype(v.dtype), v, preferred_element_type=F32)
    return m_new, acc


def _softmax_attend(q, k, v, sink=None):
    s = _qk(q, k)
    m = jnp.max(s, axis=-1, keepdims=True)
    if sink is not None:
        m = jnp.maximum(m, sink)
    p = jnp.exp2(s - m)
    l = jnp.sum(p, axis=-1, keepdims=True)
    if sink is not None:
        l = l + jnp.exp2(sink - m)
    return jnp.dot(p.astype(v.dtype), v, preferred_element_type=F32) / l


def _flash_kernel(q_ref, k_ref, v_ref, o_ref, *, shared_kv, tk):
    tq = q_ref.shape[2]
    dv = o_ref.shape[2] // 2
    heads = ((q_ref[0, 0], 0), (q_ref[0, 1], 0 if shared_kv else 1))
    carry = [(jnp.full((tq, 1), NEG_INF, F32), jnp.zeros((tq, v_ref.shape[3]), F32)) for _ in heads]
    for off in range(0, k_ref.shape[2], tk):
        carry = [_online_step(q, k_ref[0, hk, off:off + tk, :], v_ref[0, hk, off:off + tk, :], *c)
                 for (q, hk), c in zip(heads, carry)]
    outs = [(acc / pltpu.roll(acc, dv, axis=1))[:, :dv] for _, acc in carry]
    o_ref[0] = jnp.concatenate(outs, axis=-1).astype(o_ref.dtype)


def _flash(q, k, v, *, shared_kv, tq, tk):
    bsz, n_heads, n_tok, dk = q.shape
    n_keys, dvp = v.shape[2], v.shape[3]
    dv = dvp // 2
    kvb = 1 if shared_kv else 2
    kv_spec = lambda width: pl.BlockSpec((1, kvb, n_keys, width), lambda b, hp, i: (b, hp, 0, 0))
    return pl.pallas_call(
        functools.partial(_flash_kernel, shared_kv=shared_kv, tk=tk),
        grid=(bsz, n_heads // 2, n_tok // tq),
        in_specs=[pl.BlockSpec((1, 2, tq, dk), lambda b, hp, i: (b, hp, i, 0)), kv_spec(dk), kv_spec(dvp)],
        out_specs=pl.BlockSpec((1, tq, 2 * dv), lambda b, hp, i: (b, i, hp)),
        out_shape=jax.ShapeDtypeStruct((bsz, n_tok, n_heads * dv), BF16),
        compiler_params=_cparams(("parallel", "parallel", "arbitrary")),
        name="flash_gqa" if shared_kv else "flash_mla",
    )(q, k, v)


def _window_kernel(sink_ref, q_ref, bias_ref, kc_ref, vc_ref, kl_ref, vl_ref, o_ref):
    tq, span = bias_ref.shape[1], bias_ref.shape[2]
    d, n_tok = q_ref.shape[3], kl_ref.shape[2]
    n_sub = q_ref.shape[2] // tq
    n_tiles = n_tok // tq
    row1 = lax.broadcasted_iota(jnp.int32, (2 * tq, 1), 0)
    for sub in range(n_sub):
        tile = pl.program_id(1) * n_sub + sub
        start = pl.multiple_of(jnp.clip(tile * tq - WINDOW, 0, n_tok - span), WINDOW)
        bias = bias_ref[jnp.where(tile == 0, 0, jnp.where(tile == n_tiles - 1, 2, 1))]
        rows = slice(sub * tq, (sub + 1) * tq)
        outs = []
        for kvh in range(kl_ref.shape[1]):
            q = q_ref[0, 2 * kvh:2 * kvh + 2, rows, :].reshape(2 * tq, d)
            s_loc = (_qk(q, kl_ref[0, kvh, pl.ds(start, span), :]).reshape(2, tq, span) + bias).reshape(2 * tq, span)
            s_ctx = _qk(q, kc_ref[0, kvh])
            sink = jnp.where(row1 < tq, sink_ref[2 * kvh], sink_ref[2 * kvh + 1]) * LOG2E
            m = jnp.maximum(jnp.maximum(jnp.max(s_loc, axis=-1, keepdims=True),
                                        jnp.max(s_ctx, axis=-1, keepdims=True)), sink)
            acc = (jnp.dot(jnp.exp2(s_loc - m).astype(BF16), vl_ref[0, kvh, pl.ds(start, span), :],
                           preferred_element_type=F32)
                   + jnp.dot(jnp.exp2(s_ctx - m).astype(BF16), vc_ref[0, kvh], preferred_element_type=F32))
            o = (acc / (pltpu.roll(acc, d, axis=1) + jnp.exp2(sink - m)))[:, :d]
            outs += [o[:tq], o[tq:]]
        o_ref[0, rows, :] = jnp.concatenate(outs, axis=-1).astype(o_ref.dtype)


def _window_attn(sink, q, k_ctx, v_ctx, k_lat, v_lat, *, tq, n_sub):
    bsz, n_heads, n_tok, d = q.shape
    tqs = tq * n_sub
    span = tq + 2 * WINDOW
    rel = jnp.arange(span)[None, None, :] - jnp.arange(tq)[None, :, None] + jnp.array([0, -WINDOW, tq - span])[:, None, None]
    bias = jnp.where(jnp.abs(rel) <= WINDOW, 0.0, NEG_INF).astype(F32)
    kv_spec = lambda a: pl.BlockSpec((1,) + a.shape[1:], lambda b, i: (b, 0, 0, 0))
    return pl.pallas_call(
        _window_kernel,
        grid=(bsz, n_tok // tqs),
        in_specs=[pl.BlockSpec(memory_space=pltpu.SMEM),
                  pl.BlockSpec((1, n_heads, tqs, d), lambda b, i: (b, 0, i, 0)),
                  pl.BlockSpec((3, tq, span), lambda b, i: (0, 0, 0)),
                  kv_spec(k_ctx), kv_spec(v_ctx), kv_spec(k_lat), kv_spec(v_lat)],
        out_specs=pl.BlockSpec((1, tqs, n_heads * d), lambda b, i: (b, i, 0)),
        out_shape=jax.ShapeDtypeStruct((bsz, n_tok, n_heads * d), BF16),
        compiler_params=_cparams(("parallel", "arbitrary")),
        name="window_gqa",
    )(sink, q, bias, k_ctx, v_ctx, k_lat, v_lat)


def _ctx_attn_kernel(sink_ref, qa_ref, ka_ref, va_ref, qb_ref, kb_ref, vb_ref, qc_ref, kc_ref, vc_ref,
                     ya_ref, yb_ref, yc_ref):
    n = qa_ref.shape[2]
    row1 = lax.broadcasted_iota(jnp.int32, (2 * n, 1), 0)

    def gqa(q_ref, k_ref, v_ref, y_ref, with_sink):
        outs = []
        for kvh in range(k_ref.shape[1]):
            q = q_ref[0, 2 * kvh:2 * kvh + 2].reshape(2 * n, q_ref.shape[3])
            sink = (jnp.where(row1 < n, sink_ref[2 * kvh], sink_ref[2 * kvh + 1]) * LOG2E
                    if with_sink else None)
            o = _softmax_attend(q, k_ref[0, kvh], v_ref[0, kvh, :, 0:HEAD_DIM], sink)
            outs += [o[:n], o[n:]]
        y_ref[0] = jnp.concatenate(outs, axis=-1).astype(y_ref.dtype)

    gqa(qa_ref, ka_ref, va_ref, ya_ref, True)
    gqa(qb_ref, kb_ref, vb_ref, yb_ref, False)
    outs = [_softmax_attend(qc_ref[0, hh], kc_ref[0, hh], vc_ref[0, hh, :, 0:C_V_DIM]) for hh in range(C_HEADS)]
    yc_ref[0] = jnp.concatenate(outs, axis=-1).astype(yc_ref.dtype)


def _ctx_attn(sink, cx):
    bsz, _, n, _ = cx[0].shape
    spec = lambda a: pl.BlockSpec((1, a.shape[1], n, a.shape[3]), lambda b: (b, 0, a.shape[2] // n - 1, 0))
    widths = (A_HEADS * HEAD_DIM, B_HEADS * HEAD_DIM, C_HEADS * C_V_DIM)
    return pl.pallas_call(
        _ctx_attn_kernel,
        grid=(bsz,),
        in_specs=[pl.BlockSpec(memory_space=pltpu.SMEM)] + [spec(a) for a in cx],
        out_specs=[pl.BlockSpec((1, n, w), lambda b: (b, 0, 0)) for w in widths],
        out_shape=[jax.ShapeDtypeStruct((bsz, n, w), BF16) for w in widths],
        compiler_params=_cparams(("parallel",)),
        name="ctx_attn",
    )(sink, *cx)


def _layer_norm(v, g, b):
    mu = jnp.mean(v, axis=-1, keepdims=True)
    d = v - mu
    var = jnp.mean(d * d, axis=-1, keepdims=True)
    return d * lax.rsqrt(var + NORM_EPS) * g + b


def _mlp_kernel(x_ref, ya_ref, yb_ref, yc_ref, mod_ref, w_out_ref, ln1_ref, w1_ref, w2_ref, ln2_ref,
                o_ref, *, ff_chunk):
    x = x_ref[0]
    g1, sh2, sc2, g2 = (mod_ref[0, r:r + 1, :] for r in (2, 3, 4, 5))
    y = (jnp.dot(ya_ref[0], w_out_ref[0:256, :], preferred_element_type=F32)
         + jnp.dot(yb_ref[0], w_out_ref[256:512, :], preferred_element_type=F32)
         + jnp.dot(yc_ref[0], w_out_ref[512:1024, :], preferred_element_type=F32))
    x1 = _layer_norm(DEEPNORM_ALPHA * x + g1 * y, ln1_ref[0:1, :], ln1_ref[1:2, :])
    h = (x1 * (1.0 + sc2) + sh2).astype(BF16)
    acc = jnp.zeros(x.shape, F32)
    for c0 in range(0, D_FF, ff_chunk):
        a = jnp.maximum(jnp.dot(h, w1_ref[:, c0:c0 + ff_chunk], preferred_element_type=F32), 0.0)
        acc = acc + jnp.dot((a * a).astype(BF16), w2_ref[c0:c0 + ff_chunk, :], preferred_element_type=F32)
    o_ref[0] = _layer_norm(DEEPNORM_ALPHA * x1 + g2 * acc, ln2_ref[0:1, :], ln2_ref[1:2, :])


def _mlp(x, ya, yb, yc, mod, wl, *, tm):
    bsz, n_tok, _ = x.shape
    tok = lambda width: pl.BlockSpec((1, tm, width), lambda b, i: (b, i, 0))
    const = lambda shape: pl.BlockSpec(shape, lambda b, i: (0, 0), pipeline_mode=pl.Buffered(1))
    return pl.pallas_call(
        functools.partial(_mlp_kernel, ff_chunk=1024),
        grid=(bsz, n_tok // tm),
        in_specs=[tok(D_MODEL), tok(256), tok(256), tok(512),
                  pl.BlockSpec((1, 6, D_MODEL), lambda b, i: (b, 0, 0)),
                  const((D_MODEL, D_MODEL)), const((2, D_MODEL)),
                  const((D_MODEL, D_FF)), const((D_FF, D_MODEL)), const((2, D_MODEL))],
        out_specs=tok(D_MODEL),
        out_shape=jax.ShapeDtypeStruct(x.shape, F32),
        compiler_params=_cparams(("parallel", "parallel")),
        name="outproj_mlp",
    )(x, ya, yb, yc, mod, wl["w_out"], wl["ln1"], wl["w_fc1"], wl["w_fc2"], wl["ln2"])


def _rope_tables(n_tok):
    def angles(rot_dim):
        t = jnp.arange(n_tok, dtype=jnp.int32)
        row, col = (t // GRID_W).astype(F32), (t % GRID_W).astype(F32)
        n_freq = rot_dim // 4
        inv = ROPE_THETA ** (-jnp.arange(n_freq, dtype=F32) / n_freq)
        ang = jnp.concatenate([row[:, None] * inv, col[:, None] * inv], axis=-1)
        return jnp.cos(ang), jnp.sin(ang)

    c64, s64 = angles(HEAD_DIM)
    c32, s32 = angles(C_ROPE_DIM)
    ones = lambda w: jnp.ones((n_tok, w), F32)
    zeros = lambda w: jnp.zeros((n_tok, w), F32)
    cos_a = jnp.tile(jnp.concatenate([c64, c64], axis=-1), (1, A_HEADS))
    sin_a = jnp.tile(jnp.concatenate([-s64, s64], axis=-1), (1, A_HEADS))
    cos_q = _mla_lanes(ones(C_NOPE_DIM), jnp.concatenate([c32, c32], axis=-1))
    sin_q = _mla_lanes(zeros(C_NOPE_DIM), jnp.concatenate([-s32, s32], axis=-1))
    return cos_a, sin_a, cos_q, sin_q


def _mla_lanes(nope, rope):
    half = C_ROPE_DIM // 2
    split = LANES // 2 - half
    pad = jnp.zeros(nope.shape[:-1] + (C_QK_PAD - C_NOPE_DIM - C_ROPE_DIM,), nope.dtype)
    return jnp.concatenate([rope[..., :half], nope[..., :split], rope[..., half:], nope[..., split:], pad], axis=-1)


def _layer_weights(l, w_in, q_norm_b, k_norm_b, mla_q_norm, mla_kv_norm, w_uq, w_uk, w_uv, w_out,
                   ln1_g, ln1_b, w_fc1, w_fc2, ln2_g, ln2_b):
    uq = w_uq[l].reshape(C_Q_RANK, C_HEADS, C_NOPE_DIM + C_ROPE_DIM)
    uq = _mla_lanes(uq[..., :C_NOPE_DIM], uq[..., C_NOPE_DIM:])
    uk = w_uk[l].reshape(C_KV_RANK, C_HEADS, C_NOPE_DIM)
    uk = _mla_lanes(uk, jnp.zeros((C_KV_RANK, C_HEADS, C_ROPE_DIM), F32))
    w_kr = _mla_lanes(jnp.zeros((D_MODEL, C_NOPE_DIM), F32), w_in[l][:, OFF_CKR:OFF_CKR + C_ROPE_DIM])
    uv = w_uv[l].reshape(C_KV_RANK, C_HEADS, C_V_DIM)
    uv = jnp.pad(uv, ((0, 0), (0, 0), (0, V_PAD - C_V_DIM)))
    lane = jnp.arange(256)
    gmat = jnp.where((lane[:, None] // HEAD_DIM) == (lane[None, :] // HEAD_DIM), 1.0 / HEAD_DIM, 0.0)
    return {
        "w_in": jnp.concatenate([w_in[l][:, :OFF_CKR], w_kr], axis=-1).astype(BF16),
        "w_uq": uq.reshape(C_Q_RANK, C_HEADS * C_QK_PAD).astype(BF16),
        "w_uk": uk.reshape(C_KV_RANK, C_HEADS * C_QK_PAD).astype(BF16),
        "w_uv": uv.reshape(C_KV_RANK, C_HEADS * V_PAD).astype(BF16),
        "gq": jnp.tile(q_norm_b[l], B_HEADS)[None, :],
        "gk": jnp.tile(k_norm_b[l], B_KV_HEADS)[None, :],
        "gmat": gmat.astype(BF16),
        "mqn": mla_q_norm[l][None, :],
        "mkvn": mla_kv_norm[l][None, :],
        "w_out": w_out[l].astype(BF16),
        "ln1": jnp.stack([ln1_g[l], ln1_b[l]]),
        "w_fc1": w_fc1[l].astype(BF16),
        "w_fc2": w_fc2[l].astype(BF16),
        "ln2": jnp.stack([ln2_g[l], ln2_b[l]]),
    }


def kernel(x, c, ctx, c_ctx, w_mod, b_mod, w_in, sink_a, q_norm_b, k_norm_b, mla_q_norm, mla_kv_norm,
           w_uq, w_uk, w_uv, w_out, ln1_g, ln1_b, w_fc1, w_fc2, ln2_g, ln2_b):
    bsz, n_tok, _ = x.shape
    n_ctx = ctx.shape[1]
    cc = jnp.concatenate([c, c_ctx[None, :], jnp.zeros((8 - bsz - 1, D_MODEL), F32)], axis=0)
    mods = _modulation(cc, w_mod, b_mod)
    tabs = _rope_tables(n_tok)
    tabs_ctx = tuple(t[:n_ctx] for t in tabs)
    xc = ctx
    for l in range(DEPTH):
        last = l == DEPTH - 1
        wl = _layer_weights(l, w_in, q_norm_b, k_norm_b, mla_q_norm, mla_kv_norm, w_uq, w_uk, w_uv, w_out,
                            ln1_g, ln1_b, w_fc1, w_fc2, ln2_g, ln2_b)
        mod_lat = mods[l, 0:bsz].reshape(bsz, 6, D_MODEL)
        mod_ctx = jnp.broadcast_to(mods[l, bsz].reshape(1, 6, D_MODEL), (bsz, 6, D_MODEL))
        n_all = n_tok + n_ctx
        lat = _inproj(x, mod_lat, wl, tabs, rope=True, tm=512, n_all=n_all)
        qa, ka, va, qb, qc = lat[0], lat[1], lat[2], lat[3], lat[6]
        cx = _inproj(xc, mod_ctx, wl, tabs_ctx, rope=False, tm=n_ctx, n_all=n_all,
                     merged=tuple(lat[o] for o in MERGED_OUTS))
        kb, vb, kc, vc = (cx[o] for o in MERGED_OUTS)
        ya = _window_attn(sink_a[l], qa, cx[1], cx[2], ka, va, tq=256, n_sub=2)
        yb = _flash(qb, kb, vb, shared_kv=True, tq=1024, tk=768)
        yc = _flash(qc, kc, vc, shared_kv=False, tq=1024, tk=768)
        x = _mlp(x, ya, yb, yc, mod_lat, wl, tm=512)
        if not last:
            yca, ycb, ycc = _ctx_attn(sink_a[l], cx)
            xc = _mlp(xc, yca, ycb, ycc, mod_ctx, wl, tm=n_ctx)
    return x
```

```python
import functools

import jax
import jax.numpy as jnp
from jax import lax
from jax.experimental import pallas as pl
from jax.experimental.pallas import tpu as pltpu

F32 = jnp.float32
BF16 = jnp.bfloat16

D_MODEL = 1024
DEPTH = 2
GRID_W = 64
HEAD_DIM = 64
WINDOW = 128
A_HEADS, A_KV_HEADS = 4, 2
B_HEADS, B_KV_HEADS = 4, 2
C_HEADS = 8
C_Q_RANK, C_KV_RANK = 256, 128
C_NOPE_DIM, C_ROPE_DIM, C_V_DIM = 64, 32, 64
D_FF = 4 * D_MODEL
ROPE_THETA = 10000.0
NORM_EPS = 1e-6
NEG_INF = -1e30
DEEPNORM_ALPHA = (2 * DEPTH) ** 0.25

OFF_AQ, OFF_AK, OFF_AV = 0, 256, 384
OFF_BQ, OFF_BK, OFF_BV = 512, 768, 896
OFF_CQ, OFF_CKV, OFF_CKR = 1024, 1280, 1408
LANES = 128
C_QK_PAD = 128
V_PAD = 128
LOG2E = 1.4426950408889634

VMEM_LIMIT = 56 * 1024 * 1024


def _cparams(sem):
    return pltpu.CompilerParams(dimension_semantics=sem, vmem_limit_bytes=VMEM_LIMIT)


def _full(shape):
    n = len(shape)
    return pl.BlockSpec(shape, lambda *_: (0,) * n)


def _mod_kernel(cc_ref, w_ref, b_ref, o_ref):
    cc = cc_ref[...]
    s = cc / (1.0 + jnp.exp(-cc))
    o_ref[0] = jnp.dot(s, w_ref[0], preferred_element_type=F32,
                       precision=lax.Precision.HIGHEST) + b_ref[0]


def _modulation(cc, w_mod, b_mod):
    n_l = w_mod.shape[0]
    tn = 1024
    return pl.pallas_call(
        _mod_kernel,
        grid=(n_l, 6 * D_MODEL // tn),
        in_specs=[pl.BlockSpec((8, D_MODEL), lambda l, j: (0, 0)),
                  pl.BlockSpec((1, D_MODEL, tn), lambda l, j: (l, 0, j)),
                  pl.BlockSpec((1, 1, tn), lambda l, j: (l, 0, j))],
        out_specs=pl.BlockSpec((1, 8, tn), lambda l, j: (l, 0, j)),
        out_shape=jax.ShapeDtypeStruct((n_l, 8, 6 * D_MODEL), F32),
        compiler_params=_cparams(("parallel", "parallel")),
        name="modulation",
    )(cc, w_mod, b_mod.reshape(n_l, 1, 6 * D_MODEL))


def _rope(x, cos, sin_signed, half):
    width = x.shape[-1]
    lane = lax.broadcasted_iota(jnp.int32, x.shape, 1)
    first = (lane % (2 * half)) < half
    nxt = pltpu.roll(x, width - half, axis=1)
    prv = pltpu.roll(x, half, axis=1)
    return x * cos + jnp.where(first, nxt, prv) * sin_signed


def _rope_halves(x, cos, sin_signed):
    return x * cos + pltpu.roll(x, LANES // 2, axis=1) * sin_signed


def _group_mean_sq(x, g_ref):
    x2 = x * x
    hi = x2.astype(BF16)
    lo = (x2 - hi.astype(F32)).astype(BF16)
    g = g_ref[...]
    return (jnp.dot(hi, g, preferred_element_type=F32) + jnp.dot(lo, g, preferred_element_type=F32))


def _row_rms(x, gain):
    ms = jnp.mean(x * x, axis=-1, keepdims=True)
    return x * lax.rsqrt(ms + NORM_EPS) * gain


N_INPROJ_IN = 16
MERGED_OUTS = (4, 5, 7, 8)


def _inproj_kernel(*refs, rope):
    (x_ref, mod_ref, w_in_ref, w_kr_ref, w_uq_ref, w_uk_ref, w_uv_ref, gq_ref, gk_ref, gmat_ref, mqn_ref, mkvn_ref,
     cos_a_ref, sin_a_ref, cos_q_ref, sin_q_ref) = refs[:N_INPROJ_IN]
    qa_ref, ka_ref, va_ref, qb_ref, kb_ref, vb_ref, qc_ref, kc_ref, vc_ref = refs[-9:]
    x = x_ref[0]
    shift, scale = mod_ref[0, 0:1, :], mod_ref[0, 1:2, :]
    h = (x * (1.0 + scale) + shift).astype(BF16)
    p = jnp.dot(h, w_in_ref[...], preferred_element_type=F32)

    def store_heads(ref, val, n_heads, width):
        for hh in range(n_heads):
            ref[0, hh] = val[:, hh * width:(hh + 1) * width].astype(ref.dtype)

    v_lanes = lax.broadcasted_iota(jnp.int32, (x.shape[0], V_PAD), 1) < HEAD_DIM

    def store_values(ref, val):
        ref[0, 0] = jnp.where(v_lanes, val, 1.0).astype(BF16)
        ref[0, 1] = jnp.where(v_lanes, pltpu.roll(val, HEAD_DIM, axis=1), 1.0).astype(BF16)

    qa = p[:, OFF_AQ:OFF_AQ + 256]
    ka = p[:, OFF_AK:OFF_AK + 128]
    if rope:
        qa = _rope(qa, cos_a_ref[...], sin_a_ref[...], HEAD_DIM // 2)
        ka = _rope(ka, cos_a_ref[:, 0:128], sin_a_ref[:, 0:128], HEAD_DIM // 2)
    store_heads(qa_ref, qa * (HEAD_DIM ** -0.5 * LOG2E), A_HEADS, HEAD_DIM)
    store_heads(ka_ref, ka, A_KV_HEADS, HEAD_DIM)
    store_values(va_ref, p[:, OFF_AV:OFF_AV + 128])

    qb = p[:, OFF_BQ:OFF_BQ + 256]
    kb = p[:, OFF_BK:OFF_BK + 128]
    qb = qb * lax.rsqrt(_group_mean_sq(qb, gmat_ref) + NORM_EPS) * gq_ref[...]
    kb = kb * lax.rsqrt(_group_mean_sq(kb, gmat_ref.at[0:128, 0:128]) + NORM_EPS) * gk_ref[...]
    if rope:
        qb = _rope(qb, cos_a_ref[...], sin_a_ref[...], HEAD_DIM // 2)
        kb = _rope(kb, cos_a_ref[:, 0:128], sin_a_ref[:, 0:128], HEAD_DIM // 2)
    store_heads(qb_ref, qb * (HEAD_DIM ** -0.5 * LOG2E), B_HEADS, HEAD_DIM)
    store_heads(kb_ref, kb, B_KV_HEADS, HEAD_DIM)
    store_values(vb_ref, p[:, OFF_BV:OFF_BV + 128])

    cq = _row_rms(p[:, OFF_CQ:OFF_CQ + C_Q_RANK], mqn_ref[...]).astype(BF16)
    ckv = _row_rms(p[:, OFF_CKV:OFF_CKV + C_KV_RANK], mkvn_ref[...]).astype(BF16)
    q = jnp.dot(cq, w_uq_ref[...], preferred_element_type=F32)
    kn = jnp.dot(ckv, w_uk_ref[...], preferred_element_type=F32)
    vc = jnp.dot(ckv, w_uv_ref[...], preferred_element_type=F32)
    kr = jnp.dot(h, w_kr_ref[...], preferred_element_type=F32)
    if rope:
        kr = _rope_halves(kr, cos_q_ref[...], sin_q_ref[...])
    c_scale = (C_NOPE_DIM + C_ROPE_DIM) ** -0.5 * LOG2E
    for hh in range(C_HEADS):
        qh = q[:, hh * C_QK_PAD:(hh + 1) * C_QK_PAD]
        if rope:
            qh = _rope_halves(qh, cos_q_ref[...], sin_q_ref[...])
        qc_ref[0, hh] = (qh * c_scale).astype(BF16)
        kc_ref[0, hh] = (kn[:, hh * C_QK_PAD:(hh + 1) * C_QK_PAD] + kr).astype(BF16)
        vc_ref[0, hh] = jnp.where(v_lanes, vc[:, hh * V_PAD:(hh + 1) * V_PAD], 1.0).astype(BF16)


def _inproj(x, mod, wl, tabs, *, rope, tm, n_all, merged=()):
    bsz, n_tok, _ = x.shape
    nt = n_tok // tm
    row_off = (n_all - n_tok) // tm if merged else 0
    out_dims = ((A_HEADS, HEAD_DIM), (A_KV_HEADS, HEAD_DIM), (A_KV_HEADS, V_PAD),
                (B_HEADS, HEAD_DIM), (B_KV_HEADS, HEAD_DIM), (B_KV_HEADS, V_PAD),
                (C_HEADS, C_QK_PAD), (C_HEADS, C_QK_PAD), (C_HEADS, V_PAD))
    tok = lambda width: pl.BlockSpec((tm, width), lambda i, b: (i, 0))

    def out_spec(o, n, width):
        off = row_off if o in MERGED_OUTS else 0
        return pl.BlockSpec((1, n, tm, width), lambda i, b: (b, 0, i + off, 0))

    def out_shape(o, n, width):
        return jax.ShapeDtypeStruct((bsz, n, n_all if o in MERGED_OUTS else n_tok, width), BF16)

    return pl.pallas_call(
        functools.partial(_inproj_kernel, rope=rope),
        grid=(nt, bsz),
        in_specs=[pl.BlockSpec((1, tm, D_MODEL), lambda i, b: (b, i, 0)),
                  pl.BlockSpec((1, 6, D_MODEL), lambda i, b: (b, 0, 0)),
                  _full((D_MODEL, OFF_CKR)), _full((D_MODEL, C_QK_PAD)), _full((C_Q_RANK, C_HEADS * C_QK_PAD)),
                  _full((C_KV_RANK, C_HEADS * C_QK_PAD)), _full((C_KV_RANK, C_HEADS * V_PAD)),
                  _full((1, 256)), _full((1, 128)), _full((256, 256)),
                  _full((1, C_Q_RANK)), _full((1, C_KV_RANK)),
                  tok(256), tok(256), tok(128), tok(128)]
                 + [pl.BlockSpec(memory_space=pl.ANY)] * len(merged),
        out_specs=[out_spec(o, *d) for o, d in enumerate(out_dims)],
        out_shape=[out_shape(o, *d) for o, d in enumerate(out_dims)],
        input_output_aliases={N_INPROJ_IN + j: o for j, o in enumerate(MERGED_OUTS)} if merged else {},
        compiler_params=_cparams(("parallel", "parallel")),
        name="inproj_rope" if rope else "inproj_ctx",
    )(x, mod, wl["w_in"], wl["w_kr"], wl["w_uq"], wl["w_uk"], wl["w_uv"], wl["gq"], wl["gk"], wl["gmat"],
      wl["mqn"], wl["mkvn"], *tabs, *merged)


def _qk(q, k):
    return lax.dot_general(q, k, (((1,), (1,)), ((), ())), preferred_element_type=F32)


def _online_step(q, k, v, m, acc):
    s = _qk(q, k)
    m_new = jnp.maximum(m, jnp.max(s, axis=-1, keepdims=True))
    p = jnp.exp2(s - m_new)
    acc = jnp.exp2(m - m_new) * acc + jnp.dot(p.astype(v.dtype), v, preferred_element_type=F32)
    return m_new, acc


def _softmax_attend(q, k, v, sink=None):
    s = _qk(q, k)
    m = jnp.max(s, axis=-1, keepdims=True)
    if sink is not None:
        m = jnp.maximum(m, sink)
    p = jnp.exp2(s - m)
    l = jnp.sum(p, axis=-1, keepdims=True)
    if sink is not None:
        l = l + jnp.exp2(sink - m)
    return jnp.dot(p.astype(v.dtype), v, preferred_element_type=F32) / l


def _flash_kernel(q_ref, k_ref, v_ref, o_ref, *, shared_kv, tk):
    tq = q_ref.shape[2]
    dv = o_ref.shape[2] // 2
    heads = ((q_ref[0, 0], 0), (q_ref[0, 1], 0 if shared_kv else 1))
    carry = [(jnp.full((tq, 1), NEG_INF, F32), jnp.zeros((tq, v_ref.shape[3]), F32)) for _ in heads]
    for off in range(0, k_ref.shape[2], tk):
        carry = [_online_step(q, k_ref[0, hk, off:off + tk, :], v_ref[0, hk, off:off + tk, :], *c)
                 for (q, hk), c in zip(heads, carry)]
    outs = [(acc / pltpu.roll(acc, dv, axis=1))[:, :dv] for _, acc in carry]
    o_ref[0] = jnp.concatenate(outs, axis=-1).astype(o_ref.dtype)


def _flash(q, k, v, *, shared_kv, tq, tk):
    bsz, n_heads, n_tok, dk = q.shape
    n_keys, dvp = v.shape[2], v.shape[3]
    dv = dvp // 2
    kvb = 1 if shared_kv else 2
    kv_spec = lambda width: pl.BlockSpec((1, kvb, n_keys, width), lambda b, hp, i: (b, hp, 0, 0))
    return pl.pallas_call(
        functools.partial(_flash_kernel, shared_kv=shared_kv, tk=tk),
        grid=(bsz, n_heads // 2, n_tok // tq),
        in_specs=[pl.BlockSpec((1, 2, tq, dk), lambda b, hp, i: (b, hp, i, 0)), kv_spec(dk), kv_spec(dvp)],
        out_specs=pl.BlockSpec((1, tq, 2 * dv), lambda b, hp, i: (b, i, hp)),
        out_shape=jax.ShapeDtypeStruct((bsz, n_tok, n_heads * dv), BF16),
        compiler_params=_cparams(("parallel", "parallel", "arbitrary")),
        name="flash_gqa" if shared_kv else "flash_mla",
    )(q, k, v)


def _window_kernel(sink_ref, q_ref, bias_ref, kc_ref, vc_ref, kl_ref, vl_ref, o_ref):
    tq, span = bias_ref.shape[1], bias_ref.shape[2]
    d, n_tok = q_ref.shape[3], kl_ref.shape[2]
    n_sub = q_ref.shape[2] // tq
    n_tiles = n_tok // tq
    row1 = lax.broadcasted_iota(jnp.int32, (2 * tq, 1), 0)
    for sub in range(n_sub):
        tile = pl.program_id(1) * n_sub + sub
        start = pl.multiple_of(jnp.clip(tile * tq - WINDOW, 0, n_tok - span), WINDOW)
        bias = bias_ref[jnp.where(tile == 0, 0, jnp.where(tile == n_tiles - 1, 2, 1))]
        rows = slice(sub * tq, (sub + 1) * tq)
        outs = []
        for kvh in range(kl_ref.shape[1]):
            q = q_ref[0, 2 * kvh:2 * kvh + 2, rows, :].reshape(2 * tq, d)
            s_loc = (_qk(q, kl_ref[0, kvh, pl.ds(start, span), :]).reshape(2, tq, span) + bias).reshape(2 * tq, span)
            s_ctx = _qk(q, kc_ref[0, kvh])
            sink = jnp.where(row1 < tq, sink_ref[2 * kvh], sink_ref[2 * kvh + 1]) * LOG2E
            m = jnp.maximum(jnp.maximum(jnp.max(s_loc, axis=-1, keepdims=True),
                                        jnp.max(s_ctx, axis=-1, keepdims=True)), sink)
            acc = (jnp.dot(jnp.exp2(s_loc - m).astype(BF16), vl_ref[0, kvh, pl.ds(start, span), :],
                           preferred_element_type=F32)
                   + jnp.dot(jnp.exp2(s_ctx - m).astype(BF16), vc_ref[0, kvh], preferred_element_type=F32))
            o = (acc / (pltpu.roll(acc, d, axis=1) + jnp.exp2(sink - m)))[:, :d]
            outs += [o[:tq], o[tq:]]
        o_ref[0, rows, :] = jnp.concatenate(outs, axis=-1).astype(o_ref.dtype)


def _window_attn(sink, q, k_ctx, v_ctx, k_lat, v_lat, *, tq, n_sub):
    bsz, n_heads, n_tok, d = q.shape
    tqs = tq * n_sub
    span = tq + 2 * WINDOW
    rel = jnp.arange(span)[None, None, :] - jnp.arange(tq)[None, :, None] + jnp.array([0, -WINDOW, tq - span])[:, None, None]
    bias = jnp.where(jnp.abs(rel) <= WINDOW, 0.0, NEG_INF).astype(F32)
    kv_spec = lambda a: pl.BlockSpec((1,) + a.shape[1:], lambda b, i: (b, 0, 0, 0))
    return pl.pallas_call(
        _window_kernel,
        grid=(bsz, n_tok // tqs),
        in_specs=[pl.BlockSpec(memory_space=pltpu.SMEM),
                  pl.BlockSpec((1, n_heads, tqs, d), lambda b, i: (b, 0, i, 0)),
                  pl.BlockSpec((3, tq, span), lambda b, i: (0, 0, 0)),
                  kv_spec(k_ctx), kv_spec(v_ctx), kv_spec(k_lat), kv_spec(v_lat)],
        out_specs=pl.BlockSpec((1, tqs, n_heads * d), lambda b, i: (b, i, 0)),
        out_shape=jax.ShapeDtypeStruct((bsz, n_tok, n_heads * d), BF16),
        compiler_params=_cparams(("parallel", "arbitrary")),
        name="window_gqa",
    )(sink, q, bias, k_ctx, v_ctx, k_lat, v_lat)


def _ctx_attn_kernel(sink_ref, qa_ref, ka_ref, va_ref, qb_ref, kb_ref, vb_ref, qc_ref, kc_ref, vc_ref,
                     ya_ref, yb_ref, yc_ref):
    n = qa_ref.shape[2]
    row1 = lax.broadcasted_iota(jnp.int32, (2 * n, 1), 0)

    def gqa(q_ref, k_ref, v_ref, y_ref, with_sink):
        outs = []
        for kvh in range(k_ref.shape[1]):
            q = q_ref[0, 2 * kvh:2 * kvh + 2].reshape(2 * n, q_ref.shape[3])
            sink = (jnp.where(row1 < n, sink_ref[2 * kvh], sink_ref[2 * kvh + 1]) * LOG2E
                    if with_sink else None)
            o = _softmax_attend(q, k_ref[0, kvh], v_ref[0, kvh, :, 0:HEAD_DIM], sink)
            outs += [o[:n], o[n:]]
        y_ref[0] = jnp.concatenate(outs, axis=-1).astype(y_ref.dtype)

    gqa(qa_ref, ka_ref, va_ref, ya_ref, True)
    gqa(qb_ref, kb_ref, vb_ref, yb_ref, False)
    outs = [_softmax_attend(qc_ref[0, hh], kc_ref[0, hh], vc_ref[0, hh, :, 0:C_V_DIM]) for hh in range(C_HEADS)]
    yc_ref[0] = jnp.concatenate(outs, axis=-1).astype(yc_ref.dtype)


def _ctx_attn(sink, cx):
    bsz, _, n, _ = cx[0].shape
    spec = lambda a: pl.BlockSpec((1, a.shape[1], n, a.shape[3]), lambda b: (b, 0, a.shape[2] // n - 1, 0))
    widths = (A_HEADS * HEAD_DIM, B_HEADS * HEAD_DIM, C_HEADS * C_V_DIM)
    return pl.pallas_call(
        _ctx_attn_kernel,
        grid=(bsz,),
        in_specs=[pl.BlockSpec(memory_space=pltpu.SMEM)] + [spec(a) for a in cx],
        out_specs=[pl.BlockSpec((1, n, w), lambda b: (b, 0, 0)) for w in widths],
        out_shape=[jax.ShapeDtypeStruct((bsz, n, w), BF16) for w in widths],
        compiler_params=_cparams(("parallel",)),
        name="ctx_attn",
    )(sink, *cx)


def _layer_norm(v, g, b):
    mu = jnp.mean(v, axis=-1, keepdims=True)
    d = v - mu
    var = jnp.mean(d * d, axis=-1, keepdims=True)
    return d * lax.rsqrt(var + NORM_EPS) * g + b


def _mlp_kernel(x_ref, ya_ref, yb_ref, yc_ref, mod_ref, w_out_ref, ln1_ref, w1_ref, w2_ref, ln2_ref,
                o_ref, *, ff_chunk, sub):
    g1, sh2, sc2, g2 = (mod_ref[0, r:r + 1, :] for r in (2, 3, 4, 5))
    tiles = [slice(r0, r0 + sub) for r0 in range(0, x_ref.shape[1], sub)]
    x1s, hs, accs = [], [], []
    for rows in tiles:
        y = (jnp.dot(ya_ref[0, rows, :], w_out_ref[0:256, :], preferred_element_type=F32)
             + jnp.dot(yb_ref[0, rows, :], w_out_ref[256:512, :], preferred_element_type=F32)
             + jnp.dot(yc_ref[0, rows, :], w_out_ref[512:1024, :], preferred_element_type=F32))
        x1s.append(_layer_norm(DEEPNORM_ALPHA * x_ref[0, rows, :] + g1 * y, ln1_ref[0:1, :], ln1_ref[1:2, :]))
        hs.append((x1s[-1] * (1.0 + sc2) + sh2).astype(BF16))
    for h in hs:
        acc = jnp.zeros((sub, D_MODEL), F32)
        for c0 in range(0, D_FF, ff_chunk):
            a = jnp.maximum(jnp.dot(h, w1_ref[:, c0:c0 + ff_chunk], preferred_element_type=F32), 0.0)
            acc = acc + jnp.dot((a * a).astype(BF16), w2_ref[c0:c0 + ff_chunk, :], preferred_element_type=F32)
        accs.append(acc)
    for rows, x1, acc in zip(tiles, x1s, accs):
        o_ref[0, rows, :] = _layer_norm(DEEPNORM_ALPHA * x1 + g2 * acc, ln2_ref[0:1, :], ln2_ref[1:2, :])


def _mlp(x, ya, yb, yc, mod, wl, *, tm):
    bsz, n_tok, _ = x.shape
    tok = lambda width: pl.BlockSpec((1, tm, width), lambda b, i: (b, i, 0))
    const = lambda shape: pl.BlockSpec(shape, lambda b, i: (0, 0), pipeline_mode=pl.Buffered(1))
    return pl.pallas_call(
        functools.partial(_mlp_kernel, ff_chunk=1024, sub=256),
        grid=(bsz, n_tok // tm),
        in_specs=[tok(D_MODEL), tok(256), tok(256), tok(512),
                  pl.BlockSpec((1, 6, D_MODEL), lambda b, i: (b, 0, 0)),
                  const((D_MODEL, D_MODEL)), const((2, D_MODEL)),
                  const((D_MODEL, D_FF)), const((D_FF, D_MODEL)), const((2, D_MODEL))],
        out_specs=tok(D_MODEL),
        out_shape=jax.ShapeDtypeStruct(x.shape, F32),
        compiler_params=_cparams(("parallel", "parallel")),
        name="outproj_mlp",
    )(x, ya, yb, yc, mod, wl["w_out"], wl["ln1"], wl["w_fc1"], wl["w_fc2"], wl["ln2"])


def _rope_tables(n_tok):
    def angles(rot_dim):
        t = jnp.arange(n_tok, dtype=jnp.int32)
        row, col = (t // GRID_W).astype(F32), (t % GRID_W).astype(F32)
        n_freq = rot_dim // 4
        inv = ROPE_THETA ** (-jnp.arange(n_freq, dtype=F32) / n_freq)
        ang = jnp.concatenate([row[:, None] * inv, col[:, None] * inv], axis=-1)
        return jnp.cos(ang), jnp.sin(ang)

    c64, s64 = angles(HEAD_DIM)
    c32, s32 = angles(C_ROPE_DIM)
    ones = lambda w: jnp.ones((n_tok, w), F32)
    zeros = lambda w: jnp.zeros((n_tok, w), F32)
    cos_a = jnp.tile(jnp.concatenate([c64, c64], axis=-1), (1, A_HEADS))
    sin_a = jnp.tile(jnp.concatenate([-s64, s64], axis=-1), (1, A_HEADS))
    cos_q = _mla_lanes(ones(C_NOPE_DIM), jnp.concatenate([c32, c32], axis=-1))
    sin_q = _mla_lanes(zeros(C_NOPE_DIM), jnp.concatenate([-s32, s32], axis=-1))
    return cos_a, sin_a, cos_q, sin_q


def _mla_lanes(nope, rope):
    half = C_ROPE_DIM // 2
    split = LANES // 2 - half
    pad = jnp.zeros(nope.shape[:-1] + (C_QK_PAD - C_NOPE_DIM - C_ROPE_DIM,), nope.dtype)
    return jnp.concatenate([rope[..., :half], nope[..., :split], rope[..., half:], nope[..., split:], pad], axis=-1)


def _layer_weights(l, w_in, q_norm_b, k_norm_b, mla_q_norm, mla_kv_norm, w_uq, w_uk, w_uv, w_out,
                   ln1_g, ln1_b, w_fc1, w_fc2, ln2_g, ln2_b):
    uq = w_uq[l].reshape(C_Q_RANK, C_HEADS, C_NOPE_DIM + C_ROPE_DIM)
    uq = _mla_lanes(uq[..., :C_NOPE_DIM], uq[..., C_NOPE_DIM:])
    uk = w_uk[l].reshape(C_KV_RANK, C_HEADS, C_NOPE_DIM)
    uk = _mla_lanes(uk, jnp.zeros((C_KV_RANK, C_HEADS, C_ROPE_DIM), F32))
    w_kr = _mla_lanes(jnp.zeros((D_MODEL, C_NOPE_DIM), F32), w_in[l][:, OFF_CKR:OFF_CKR + C_ROPE_DIM])
    uv = w_uv[l].reshape(C_KV_RANK, C_HEADS, C_V_DIM)
    uv = jnp.pad(uv, ((0, 0), (0, 0), (0, V_PAD - C_V_DIM)))
    lane = jnp.arange(256)
    gmat = jnp.where((lane[:, None] // HEAD_DIM) == (lane[None, :] // HEAD_DIM), 1.0 / HEAD_DIM, 0.0)
    return {
        "w_in": w_in[l][:, :OFF_CKR].astype(BF16),
        "w_kr": w_kr.astype(BF16),
        "w_uq": uq.reshape(C_Q_RANK, C_HEADS * C_QK_PAD).astype(BF16),
        "w_uk": uk.reshape(C_KV_RANK, C_HEADS * C_QK_PAD).astype(BF16),
        "w_uv": uv.reshape(C_KV_RANK, C_HEADS * V_PAD).astype(BF16),
        "gq": jnp.tile(q_norm_b[l], B_HEADS)[None, :],
        "gk": jnp.tile(k_norm_b[l], B_KV_HEADS)[None, :],
        "gmat": gmat.astype(BF16),
        "mqn": mla_q_norm[l][None, :],
        "mkvn": mla_kv_norm[l][None, :],
        "w_out": w_out[l].astype(BF16),
        "ln1": jnp.stack([ln1_g[l], ln1_b[l]]),
        "w_fc1": w_fc1[l].astype(BF16),
        "w_fc2": w_fc2[l].astype(BF16),
        "ln2": jnp.stack([ln2_g[l], ln2_b[l]]),
    }


def kernel(x, c, ctx, c_ctx, w_mod, b_mod, w_in, sink_a, q_norm_b, k_norm_b, mla_q_norm, mla_kv_norm,
           w_uq, w_uk, w_uv, w_out, ln1_g, ln1_b, w_fc1, w_fc2, ln2_g, ln2_b):
    bsz, n_tok, _ = x.shape
    n_ctx = ctx.shape[1]
    cc = jnp.concatenate([c, c_ctx[None, :], jnp.zeros((8 - bsz - 1, D_MODEL), F32)], axis=0)
    mods = _modulation(cc, w_mod, b_mod)
    tabs = _rope_tables(n_tok)
    tabs_ctx = tuple(t[:n_ctx] for t in tabs)
    xc = ctx
    for l in range(DEPTH):
        last = l == DEPTH - 1
        wl = _layer_weights(l, w_in, q_norm_b, k_norm_b, mla_q_norm, mla_kv_norm, w_uq, w_uk, w_uv, w_out,
                            ln1_g, ln1_b, w_fc1, w_fc2, ln2_g, ln2_b)
        mod_lat = mods[l, 0:bsz].reshape(bsz, 6, D_MODEL)
        mod_ctx = jnp.broadcast_to(mods[l, bsz].reshape(1, 6, D_MODEL), (bsz, 6, D_MODEL))
        n_all = n_tok + n_ctx
        lat = _inproj(x, mod_lat, wl, tabs, rope=True, tm=512, n_all=n_all)
        qa, ka, va, qb, qc = lat[0], lat[1], lat[2], lat[3], lat[6]
        cx = _inproj(xc, mod_ctx, wl, tabs_ctx, rope=False, tm=n_ctx, n_all=n_all,
                     merged=tuple(lat[o] for o in MERGED_OUTS))
        kb, vb, kc, vc = (cx[o] for o in MERGED_OUTS)
        ya = _window_attn(sink_a[l], qa, cx[1], cx[2], ka, va, tq=256, n_sub=2)
        yb = _flash(qb, kb, vb, shared_kv=True, tq=1024, tk=768)
        yc = _flash(qc, kc, vc, shared_kv=False, tq=1024, tk=768)
        x = _mlp(x, ya, yb, yc, mod_lat, wl, tm=512)
        if not last:
            yca, ycb, ycc = _ctx_attn(sink_a[l], cx)
            xc = _mlp(xc, yca, ycb, ycc, mod_ctx, wl, tm=n_ctx)
    return x
```

```python
import functools

import jax
import jax.numpy as jnp
from jax import lax
from jax.experimental import pallas as pl
from jax.experimental.pallas import tpu as pltpu

F32 = jnp.float32
BF16 = jnp.bfloat16

D_MODEL = 1024
DEPTH = 2
GRID_W = 64
HEAD_DIM = 64
WINDOW = 128
A_HEADS, A_KV_HEADS = 4, 2
B_HEADS, B_KV_HEADS = 4, 2
C_HEADS = 8
C_Q_RANK, C_KV_RANK = 256, 128
C_NOPE_DIM, C_ROPE_DIM, C_V_DIM = 64, 32, 64
D_FF = 4 * D_MODEL
ROPE_THETA = 10000.0
NORM_EPS = 1e-6
NEG_INF = -1e30
DEEPNORM_ALPHA = (2 * DEPTH) ** 0.25

OFF_AQ, OFF_AK, OFF_AV = 0, 256, 384
OFF_BQ, OFF_BK, OFF_BV = 512, 768, 896
OFF_CQ, OFF_CKV, OFF_CKR = 1024, 1280, 1408
LANES = 128
C_QK_PAD = 128
V_PAD = 128
LOG2E = 1.4426950408889634

VMEM_LIMIT = 56 * 1024 * 1024

INPROJ_TM = 512
FLASH_TQ, FLASH_TK = 1024, 768
WINDOW_TQ, WINDOW_TILES_PER_STEP = 256, 4
MLP_TM, MLP_SUB, MLP_FF_CHUNK = 512, 256, 1024
MOD_TN = 1024


def _cparams(sem):
    return pltpu.CompilerParams(dimension_semantics=sem, vmem_limit_bytes=VMEM_LIMIT)


def _full(shape):
    n = len(shape)
    return pl.BlockSpec(shape, lambda *_: (0,) * n)


def _mod_kernel(cc_ref, w_ref, b_ref, o_ref):
    cc = cc_ref[...]
    s = cc / (1.0 + jnp.exp(-cc))
    o_ref[0] = jnp.dot(s, w_ref[0], preferred_element_type=F32,
                       precision=lax.Precision.HIGHEST) + b_ref[0]


def _modulation(cc, w_mod, b_mod):
    n_l = w_mod.shape[0]
    tn = MOD_TN
    return pl.pallas_call(
        _mod_kernel,
        grid=(n_l, 6 * D_MODEL // tn),
        in_specs=[pl.BlockSpec((8, D_MODEL), lambda l, j: (0, 0)),
                  pl.BlockSpec((1, D_MODEL, tn), lambda l, j: (l, 0, j)),
                  pl.BlockSpec((1, 1, tn), lambda l, j: (l, 0, j))],
        out_specs=pl.BlockSpec((1, 8, tn), lambda l, j: (l, 0, j)),
        out_shape=jax.ShapeDtypeStruct((n_l, 8, 6 * D_MODEL), F32),
        compiler_params=_cparams(("parallel", "parallel")),
        name="modulation",
    )(cc, w_mod, b_mod.reshape(n_l, 1, 6 * D_MODEL))


def _rope(x, cos, sin_signed, half):
    width = x.shape[-1]
    lane = lax.broadcasted_iota(jnp.int32, x.shape, 1)
    first = (lane % (2 * half)) < half
    nxt = pltpu.roll(x, width - half, axis=1)
    prv = pltpu.roll(x, half, axis=1)
    return x * cos + jnp.where(first, nxt, prv) * sin_signed


def _rope_halves(x, cos, sin_signed):
    return x * cos + pltpu.roll(x, LANES // 2, axis=1) * sin_signed


def _group_mean_sq(x, g_ref):
    x2 = x * x
    hi = x2.astype(BF16)
    lo = (x2 - hi.astype(F32)).astype(BF16)
    g = g_ref[...]
    return (jnp.dot(hi, g, preferred_element_type=F32) + jnp.dot(lo, g, preferred_element_type=F32))


def _row_rms(x, gain):
    ms = jnp.mean(x * x, axis=-1, keepdims=True)
    return x * lax.rsqrt(ms + NORM_EPS) * gain


N_INPROJ_IN = 16
MERGED_OUTS = (4, 5, 7, 8)


def _inproj_kernel(*refs, rope):
    (x_ref, mod_ref, w_in_ref, w_kr_ref, w_uq_ref, w_uk_ref, w_uv_ref, gq_ref, gk_ref, gmat_ref, mqn_ref, mkvn_ref,
     cos_a_ref, sin_a_ref, cos_q_ref, sin_q_ref) = refs[:N_INPROJ_IN]
    qa_ref, ka_ref, va_ref, qb_ref, kb_ref, vb_ref, qc_ref, kc_ref, vc_ref = refs[-9:]
    x = x_ref[0]
    shift, scale = mod_ref[0, 0:1, :], mod_ref[0, 1:2, :]
    h = (x * (1.0 + scale) + shift).astype(BF16)
    p = jnp.dot(h, w_in_ref[...], preferred_element_type=F32)

    def store_heads(ref, val, n_heads, width):
        for hh in range(n_heads):
            ref[0, hh] = val[:, hh * width:(hh + 1) * width].astype(ref.dtype)

    v_lanes = lax.broadcasted_iota(jnp.int32, (x.shape[0], V_PAD), 1) < HEAD_DIM

    def store_values(ref, val):
        ref[0, 0] = jnp.where(v_lanes, val, 1.0).astype(BF16)
        ref[0, 1] = jnp.where(v_lanes, pltpu.roll(val, HEAD_DIM, axis=1), 1.0).astype(BF16)

    qa = p[:, OFF_AQ:OFF_AQ + 256]
    ka = p[:, OFF_AK:OFF_AK + 128]
    if rope:
        qa = _rope(qa, cos_a_ref[...], sin_a_ref[...], HEAD_DIM // 2)
        ka = _rope(ka, cos_a_ref[:, 0:128], sin_a_ref[:, 0:128], HEAD_DIM // 2)
    store_heads(qa_ref, qa * (HEAD_DIM ** -0.5 * LOG2E), A_HEADS, HEAD_DIM)
    store_heads(ka_ref, ka, A_KV_HEADS, HEAD_DIM)
    store_values(va_ref, p[:, OFF_AV:OFF_AV + 128])

    qb = p[:, OFF_BQ:OFF_BQ + 256]
    kb = p[:, OFF_BK:OFF_BK + 128]
    qb = qb * lax.rsqrt(_group_mean_sq(qb, gmat_ref) + NORM_EPS) * gq_ref[...]
    kb = kb * lax.rsqrt(_group_mean_sq(kb, gmat_ref.at[0:128, 0:128]) + NORM_EPS) * gk_ref[...]
    if rope:
        qb = _rope(qb, cos_a_ref[...], sin_a_ref[...], HEAD_DIM // 2)
        kb = _rope(kb, cos_a_ref[:, 0:128], sin_a_ref[:, 0:128], HEAD_DIM // 2)
    store_heads(qb_ref, qb * (HEAD_DIM ** -0.5 * LOG2E), B_HEADS, HEAD_DIM)
    store_heads(kb_ref, kb, B_KV_HEADS, HEAD_DIM)
    store_values(vb_ref, p[:, OFF_BV:OFF_BV + 128])

    cq = _row_rms(p[:, OFF_CQ:OFF_CQ + C_Q_RANK], mqn_ref[...]).astype(BF16)
    ckv = _row_rms(p[:, OFF_CKV:OFF_CKV + C_KV_RANK], mkvn_ref[...]).astype(BF16)
    q = jnp.dot(cq, w_uq_ref[...], preferred_element_type=F32)
    kn = jnp.dot(ckv, w_uk_ref[...], preferred_element_type=F32)
    vc = jnp.dot(ckv, w_uv_ref[...], preferred_element_type=F32)
    kr = jnp.dot(h, w_kr_ref[...], preferred_element_type=F32)
    if rope:
        kr = _rope_halves(kr, cos_q_ref[...], sin_q_ref[...])
    c_scale = (C_NOPE_DIM + C_ROPE_DIM) ** -0.5 * LOG2E
    for hh in range(C_HEADS):
        qh = q[:, hh * C_QK_PAD:(hh + 1) * C_QK_PAD]
        if rope:
            qh = _rope_halves(qh, cos_q_ref[...], sin_q_ref[...])
        qc_ref[0, hh] = (qh * c_scale).astype(BF16)
        kc_ref[0, hh] = (kn[:, hh * C_QK_PAD:(hh + 1) * C_QK_PAD] + kr).astype(BF16)
        vc_ref[0, hh] = jnp.where(v_lanes, vc[:, hh * V_PAD:(hh + 1) * V_PAD], 1.0).astype(BF16)


def _inproj(x, mod, wl, tabs, *, rope, tm, n_all, merged=()):
    bsz, n_tok, _ = x.shape
    assert n_tok % tm == 0 and (not merged or (n_all - n_tok) % tm == 0)
    nt = n_tok // tm
    row_off = (n_all - n_tok) // tm if merged else 0
    out_dims = ((A_HEADS, HEAD_DIM), (A_KV_HEADS, HEAD_DIM), (A_KV_HEADS, V_PAD),
                (B_HEADS, HEAD_DIM), (B_KV_HEADS, HEAD_DIM), (B_KV_HEADS, V_PAD),
                (C_HEADS, C_QK_PAD), (C_HEADS, C_QK_PAD), (C_HEADS, V_PAD))
    tok = lambda width: pl.BlockSpec((tm, width), lambda i, b: (i, 0))

    def out_spec(o, n, width):
        off = row_off if o in MERGED_OUTS else 0
        return pl.BlockSpec((1, n, tm, width), lambda i, b: (b, 0, i + off, 0))

    def out_shape(o, n, width):
        return jax.ShapeDtypeStruct((bsz, n, n_all if o in MERGED_OUTS else n_tok, width), BF16)

    return pl.pallas_call(
        functools.partial(_inproj_kernel, rope=rope),
        grid=(nt, bsz),
        in_specs=[pl.BlockSpec((1, tm, D_MODEL), lambda i, b: (b, i, 0)),
                  pl.BlockSpec((1, 6, D_MODEL), lambda i, b: (b, 0, 0)),
                  _full((D_MODEL, OFF_CKR)), _full((D_MODEL, C_QK_PAD)), _full((C_Q_RANK, C_HEADS * C_QK_PAD)),
                  _full((C_KV_RANK, C_HEADS * C_QK_PAD)), _full((C_KV_RANK, C_HEADS * V_PAD)),
                  _full((1, 256)), _full((1, 128)), _full((256, 256)),
                  _full((1, C_Q_RANK)), _full((1, C_KV_RANK)),
                  tok(256), tok(256), tok(128), tok(128)]
                 + [pl.BlockSpec(memory_space=pl.ANY)] * len(merged),
        out_specs=[out_spec(o, *d) for o, d in enumerate(out_dims)],
        out_shape=[out_shape(o, *d) for o, d in enumerate(out_dims)],
        input_output_aliases={N_INPROJ_IN + j: o for j, o in enumerate(MERGED_OUTS)} if merged else {},
        compiler_params=_cparams(("parallel", "parallel")),
        name="inproj_rope" if rope else "inproj_ctx",
    )(x, mod, wl["w_in"], wl["w_kr"], wl["w_uq"], wl["w_uk"], wl["w_uv"], wl["gq"], wl["gk"], wl["gmat"],
      wl["mqn"], wl["mkvn"], *tabs, *merged)


def _qk(q, k):
    return lax.dot_general(q, k, (((1,), (1,)), ((), ())), preferred_element_type=F32)


def _online_step(q, k, v, m, acc):
    s = _qk(q, k)
    m_new = jnp.maximum(m, jnp.max(s, axis=-1, keepdims=True))
    p = jnp.exp2(s - m_new)
    acc = jnp.exp2(m - m_new) * acc + jnp.dot(p.astype(v.dtype), v, preferred_element_type=F32)
    return m_new, acc


def _softmax_attend(q, k, v, sink=None):
    s = _qk(q, k)
    m = jnp.max(s, axis=-1, keepdims=True)
    if sink is not None:
        m = jnp.maximum(m, sink)
    p = jnp.exp2(s - m)
    l = jnp.sum(p, axis=-1, keepdims=True)
    if sink is not None:
        l = l + jnp.exp2(sink - m)
    return jnp.dot(p.astype(v.dtype), v, preferred_element_type=F32) / l


def _flash_kernel(q_ref, k_ref, v_ref, o_ref, *, shared_kv, tk):
    tq = q_ref.shape[2]
    dv = o_ref.shape[2] // 2
    heads = ((q_ref[0, 0], 0), (q_ref[0, 1], 0 if shared_kv else 1))
    carry = [(jnp.full((tq, 1), NEG_INF, F32), jnp.zeros((tq, v_ref.shape[3]), F32)) for _ in heads]
    for off in range(0, k_ref.shape[2], tk):
        carry = [_online_step(q, k_ref[0, hk, off:off + tk, :], v_ref[0, hk, off:off + tk, :], *c)
                 for (q, hk), c in zip(heads, carry)]
    outs = [(acc / pltpu.roll(acc, dv, axis=1))[:, :dv] for _, acc in carry]
    o_ref[0] = jnp.concatenate(outs, axis=-1).astype(o_ref.dtype)


def _flash(q, k, v, *, shared_kv, tq, tk):
    bsz, n_heads, n_tok, dk = q.shape
    n_keys, dvp = v.shape[2], v.shape[3]
    dv = dvp // 2
    assert n_tok % tq == 0 and n_keys % tk == 0
    kvb = 1 if shared_kv else 2
    kv_spec = lambda width: pl.BlockSpec((1, kvb, n_keys, width), lambda b, hp, i: (b, hp, 0, 0))
    return pl.pallas_call(
        functools.partial(_flash_kernel, shared_kv=shared_kv, tk=tk),
        grid=(bsz, n_heads // 2, n_tok // tq),
        in_specs=[pl.BlockSpec((1, 2, tq, dk), lambda b, hp, i: (b, hp, i, 0)), kv_spec(dk), kv_spec(dvp)],
        out_specs=pl.BlockSpec((1, tq, 2 * dv), lambda b, hp, i: (b, i, hp)),
        out_shape=jax.ShapeDtypeStruct((bsz, n_tok, n_heads * dv), BF16),
        compiler_params=_cparams(("parallel", "parallel", "arbitrary")),
        name="flash_gqa" if shared_kv else "flash_mla",
    )(q, k, v)


def _window_kernel(sink_ref, q_ref, bias_ref, kc_ref, vc_ref, kl_ref, vl_ref, o_ref):
    tq, span = bias_ref.shape[1], bias_ref.shape[2]
    d, n_tok = q_ref.shape[3], kl_ref.shape[2]
    n_sub = q_ref.shape[2] // tq
    n_tiles = n_tok // tq
    row1 = lax.broadcasted_iota(jnp.int32, (2 * tq, 1), 0)
    for sub in range(n_sub):
        tile = pl.program_id(1) * n_sub + sub
        start = pl.multiple_of(jnp.clip(tile * tq - WINDOW, 0, n_tok - span), WINDOW)
        bias = bias_ref[jnp.where(tile == 0, 0, jnp.where(tile == n_tiles - 1, 2, 1))]
        rows = slice(sub * tq, (sub + 1) * tq)
        outs = []
        for kvh in range(kl_ref.shape[1]):
            q = q_ref[0, 2 * kvh:2 * kvh + 2, rows, :].reshape(2 * tq, d)
            s_loc = (_qk(q, kl_ref[0, kvh, pl.ds(start, span), :]).reshape(2, tq, span) + bias).reshape(2 * tq, span)
            s_ctx = _qk(q, kc_ref[0, kvh])
            sink = jnp.where(row1 < tq, sink_ref[2 * kvh], sink_ref[2 * kvh + 1]) * LOG2E
            m = jnp.maximum(jnp.maximum(jnp.max(s_loc, axis=-1, keepdims=True),
                                        jnp.max(s_ctx, axis=-1, keepdims=True)), sink)
            acc = (jnp.dot(jnp.exp2(s_loc - m).astype(BF16), vl_ref[0, kvh, pl.ds(start, span), :],
                           preferred_element_type=F32)
                   + jnp.dot(jnp.exp2(s_ctx - m).astype(BF16), vc_ref[0, kvh], preferred_element_type=F32))
            o = (acc / (pltpu.roll(acc, d, axis=1) + jnp.exp2(sink - m)))[:, :d]
            outs += [o[:tq], o[tq:]]
        o_ref[0, rows, :] = jnp.concatenate(outs, axis=-1).astype(o_ref.dtype)


def _window_attn(sink, q, k_ctx, v_ctx, k_lat, v_lat, *, tq, n_sub):
    bsz, n_heads, n_tok, d = q.shape
    tqs = tq * n_sub
    span = tq + 2 * WINDOW
    assert n_tok % tqs == 0 and n_tok // tq >= 2 and span <= n_tok and tq % WINDOW == 0
    rel = jnp.arange(span)[None, None, :] - jnp.arange(tq)[None, :, None] + jnp.array([0, -WINDOW, tq - span])[:, None, None]
    bias = jnp.where(jnp.abs(rel) <= WINDOW, 0.0, NEG_INF).astype(F32)
    kv_spec = lambda a: pl.BlockSpec((1,) + a.shape[1:], lambda b, i: (b, 0, 0, 0))
    return pl.pallas_call(
        _window_kernel,
        grid=(bsz, n_tok // tqs),
        in_specs=[pl.BlockSpec(memory_space=pltpu.SMEM),
                  pl.BlockSpec((1, n_heads, tqs, d), lambda b, i: (b, 0, i, 0)),
                  pl.BlockSpec((3, tq, span), lambda b, i: (0, 0, 0)),
                  kv_spec(k_ctx), kv_spec(v_ctx), kv_spec(k_lat), kv_spec(v_lat)],
        out_specs=pl.BlockSpec((1, tqs, n_heads * d), lambda b, i: (b, i, 0)),
        out_shape=jax.ShapeDtypeStruct((bsz, n_tok, n_heads * d), BF16),
        compiler_params=_cparams(("parallel", "arbitrary")),
        name="window_gqa",
    )(sink, q, bias, k_ctx, v_ctx, k_lat, v_lat)


def _ctx_attn_kernel(sink_ref, qa_ref, ka_ref, va_ref, qb_ref, kb_ref, vb_ref, qc_ref, kc_ref, vc_ref,
                     ya_ref, yb_ref, yc_ref):
    n = qa_ref.shape[2]
    row1 = lax.broadcasted_iota(jnp.int32, (2 * n, 1), 0)

    def gqa(q_ref, k_ref, v_ref, y_ref, with_sink):
        outs = []
        for kvh in range(k_ref.shape[1]):
            q = q_ref[0, 2 * kvh:2 * kvh + 2].reshape(2 * n, q_ref.shape[3])
            sink = (jnp.where(row1 < n, sink_ref[2 * kvh], sink_ref[2 * kvh + 1]) * LOG2E
                    if with_sink else None)
            o = _softmax_attend(q, k_ref[0, kvh], v_ref[0, kvh, :, 0:HEAD_DIM], sink)
            outs += [o[:n], o[n:]]
        y_ref[0] = jnp.concatenate(outs, axis=-1).astype(y_ref.dtype)

    gqa(qa_ref, ka_ref, va_ref, ya_ref, True)
    gqa(qb_ref, kb_ref, vb_ref, yb_ref, False)
    outs = [_softmax_attend(qc_ref[0, hh], kc_ref[0, hh], vc_ref[0, hh, :, 0:C_V_DIM]) for hh in range(C_HEADS)]
    yc_ref[0] = jnp.concatenate(outs, axis=-1).astype(yc_ref.dtype)


def _ctx_attn(sink, cx):
    bsz, _, n, _ = cx[0].shape
    spec = lambda a: pl.BlockSpec((1, a.shape[1], n, a.shape[3]), lambda b: (b, 0, a.shape[2] // n - 1, 0))
    widths = (A_HEADS * HEAD_DIM, B_HEADS * HEAD_DIM, C_HEADS * C_V_DIM)
    return pl.pallas_call(
        _ctx_attn_kernel,
        grid=(bsz,),
        in_specs=[pl.BlockSpec(memory_space=pltpu.SMEM)] + [spec(a) for a in cx],
        out_specs=[pl.BlockSpec((1, n, w), lambda b: (b, 0, 0)) for w in widths],
        out_shape=[jax.ShapeDtypeStruct((bsz, n, w), BF16) for w in widths],
        compiler_params=_cparams(("parallel",)),
        name="ctx_attn",
    )(sink, *cx)


def _layer_norm(v, g, b):
    mu = jnp.mean(v, axis=-1, keepdims=True)
    d = v - mu
    var = jnp.mean(d * d, axis=-1, keepdims=True)
    return d * lax.rsqrt(var + NORM_EPS) * g + b


def _mlp_kernel(x_ref, ya_ref, yb_ref, yc_ref, mod_ref, w_out_ref, ln1_ref, w1_ref, w2_ref, ln2_ref,
                o_ref, *, ff_chunk, sub):
    g1, sh2, sc2, g2 = (mod_ref[0, r:r + 1, :] for r in (2, 3, 4, 5))
    tiles = [slice(r0, r0 + sub) for r0 in range(0, x_ref.shape[1], sub)]
    x1s, hs, accs = [], [], []
    for rows in tiles:
        y = (jnp.dot(ya_ref[0, rows, :], w_out_ref[0:256, :], preferred_element_type=F32)
             + jnp.dot(yb_ref[0, rows, :], w_out_ref[256:512, :], preferred_element_type=F32)
             + jnp.dot(yc_ref[0, rows, :], w_out_ref[512:1024, :], preferred_element_type=F32))
        x1s.append(_layer_norm(DEEPNORM_ALPHA * x_ref[0, rows, :] + g1 * y, ln1_ref[0:1, :], ln1_ref[1:2, :]))
        hs.append((x1s[-1] * (1.0 + sc2) + sh2).astype(BF16))
    for h in hs:
        acc = jnp.zeros((sub, D_MODEL), F32)
        for c0 in range(0, D_FF, ff_chunk):
            a = jnp.maximum(jnp.dot(h, w1_ref[:, c0:c0 + ff_chunk], preferred_element_type=F32), 0.0)
            acc = acc + jnp.dot((a * a).astype(BF16), w2_ref[c0:c0 + ff_chunk, :], preferred_element_type=F32)
        accs.append(acc)
    for rows, x1, acc in zip(tiles, x1s, accs):
        o_ref[0, rows, :] = _layer_norm(DEEPNORM_ALPHA * x1 + g2 * acc, ln2_ref[0:1, :], ln2_ref[1:2, :])


def _mlp(x, ya, yb, yc, mod, wl, *, tm):
    bsz, n_tok, _ = x.shape
    tok = lambda width: pl.BlockSpec((1, tm, width), lambda b, i: (b, i, 0))
    const = lambda shape: pl.BlockSpec(shape, lambda b, i: (0, 0), pipeline_mode=pl.Buffered(1))
    return pl.pallas_call(
        functools.partial(_mlp_kernel, ff_chunk=MLP_FF_CHUNK, sub=min(tm, MLP_SUB)),
        grid=(bsz, n_tok // tm),
        in_specs=[tok(D_MODEL), tok(256), tok(256), tok(512),
                  pl.BlockSpec((1, 6, D_MODEL), lambda b, i: (b, 0, 0)),
                  const((D_MODEL, D_MODEL)), const((2, D_MODEL)),
                  const((D_MODEL, D_FF)), const((D_FF, D_MODEL)), const((2, D_MODEL))],
        out_specs=tok(D_MODEL),
        out_shape=jax.ShapeDtypeStruct(x.shape, F32),
        compiler_params=_cparams(("parallel", "parallel")),
        name="outproj_mlp",
    )(x, ya, yb, yc, mod, wl["w_out"], wl["ln1"], wl["w_fc1"], wl["w_fc2"], wl["ln2"])


def _rope_tables(n_tok):
    n_rows = n_tok // GRID_W

    def parts(rot_dim):
        n_freq = rot_dim // 4
        inv = ROPE_THETA ** (-jnp.arange(n_freq, dtype=F32) / n_freq)
        ang_r = jnp.arange(n_rows, dtype=F32)[:, None] * inv
        ang_c = jnp.arange(GRID_W, dtype=F32)[:, None] * inv
        z_r, z_c = jnp.zeros_like(ang_r), jnp.zeros_like(ang_c)
        split = lambda f: (jnp.concatenate([f(ang_r), z_r], axis=-1), jnp.concatenate([z_c, f(ang_c)], axis=-1))
        return split(jnp.cos), split(jnp.sin)

    def expand(r_part, c_part):
        return (r_part[:, None, :] + c_part[None, :, :]).reshape(n_tok, r_part.shape[-1])

    lay_a = lambda v, sign: jnp.tile(jnp.concatenate([sign * v, v], axis=-1), (1, A_HEADS))
    lay_q = lambda v, sign, nope: _mla_lanes(jnp.full(v.shape[:-1] + (C_NOPE_DIM,), nope, F32),
                                             jnp.concatenate([sign * v, v], axis=-1))
    (c64r, c64c), (s64r, s64c) = parts(HEAD_DIM)
    (c32r, c32c), (s32r, s32c) = parts(C_ROPE_DIM)
    cos_a = expand(lay_a(c64r, 1.0), lay_a(c64c, 1.0))
    sin_a = expand(lay_a(s64r, -1.0), lay_a(s64c, -1.0))
    cos_q = expand(lay_q(c32r, 1.0, 0.0), lay_q(c32c, 1.0, 1.0))
    sin_q = expand(lay_q(s32r, -1.0, 0.0), lay_q(s32c, -1.0, 0.0))
    return cos_a, sin_a, cos_q, sin_q


def _mla_lanes(nope, rope):
    half = C_ROPE_DIM // 2
    split = LANES // 2 - half
    pad = jnp.zeros(nope.shape[:-1] + (C_QK_PAD - C_NOPE_DIM - C_ROPE_DIM,), nope.dtype)
    return jnp.concatenate([rope[..., :half], nope[..., :split], rope[..., half:], nope[..., split:], pad], axis=-1)


def _layer_weights(l, w_in, q_norm_b, k_norm_b, mla_q_norm, mla_kv_norm, w_uq, w_uk, w_uv, w_out,
                   ln1_g, ln1_b, w_fc1, w_fc2, ln2_g, ln2_b):
    uq = w_uq[l].reshape(C_Q_RANK, C_HEADS, C_NOPE_DIM + C_ROPE_DIM)
    uq = _mla_lanes(uq[..., :C_NOPE_DIM], uq[..., C_NOPE_DIM:])
    uk = w_uk[l].reshape(C_KV_RANK, C_HEADS, C_NOPE_DIM)
    uk = _mla_lanes(uk, jnp.zeros((C_KV_RANK, C_HEADS, C_ROPE_DIM), F32))
    w_kr = _mla_lanes(jnp.zeros((D_MODEL, C_NOPE_DIM), F32), w_in[l][:, OFF_CKR:OFF_CKR + C_ROPE_DIM])
    uv = w_uv[l].reshape(C_KV_RANK, C_HEADS, C_V_DIM)
    uv = jnp.pad(uv, ((0, 0), (0, 0), (0, V_PAD - C_V_DIM)))
    lane = jnp.arange(256)
    gmat = jnp.where((lane[:, None] // HEAD_DIM) == (lane[None, :] // HEAD_DIM), 1.0 / HEAD_DIM, 0.0)
    return {
        "w_in": w_in[l][:, :OFF_CKR].astype(BF16),
        "w_kr": w_kr.astype(BF16),
        "w_uq": uq.reshape(C_Q_RANK, C_HEADS * C_QK_PAD).astype(BF16),
        "w_uk": uk.reshape(C_KV_RANK, C_HEADS * C_QK_PAD).astype(BF16),
        "w_uv": uv.reshape(C_KV_RANK, C_HEADS * V_PAD).astype(BF16),
        "gq": jnp.tile(q_norm_b[l], B_HEADS)[None, :],
        "gk": jnp.tile(k_norm_b[l], B_KV_HEADS)[None, :],
        "gmat": gmat.astype(BF16),
        "mqn": mla_q_norm[l][None, :],
        "mkvn": mla_kv_norm[l][None, :],
        "w_out": w_out[l].astype(BF16),
        "ln1": jnp.stack([ln1_g[l], ln1_b[l]]),
        "w_fc1": w_fc1[l].astype(BF16),
        "w_fc2": w_fc2[l].astype(BF16),
        "ln2": jnp.stack([ln2_g[l], ln2_b[l]]),
    }


def kernel(x, c, ctx, c_ctx, w_mod, b_mod, w_in, sink_a, q_norm_b, k_norm_b, mla_q_norm, mla_kv_norm,
           w_uq, w_uk, w_uv, w_out, ln1_g, ln1_b, w_fc1, w_fc2, ln2_g, ln2_b):
    bsz, n_tok, _ = x.shape
    n_ctx = ctx.shape[1]
    cc = jnp.concatenate([c, c_ctx[None, :], jnp.zeros((8 - bsz - 1, D_MODEL), F32)], axis=0)
    mods = _modulation(cc, w_mod, b_mod)
    tabs = _rope_tables(n_tok)
    tabs_ctx = tuple(t[:n_ctx] for t in tabs)
    xc = ctx
    for l in range(DEPTH):
        last = l == DEPTH - 1
        wl = _layer_weights(l, w_in, q_norm_b, k_norm_b, mla_q_norm, mla_kv_norm, w_uq, w_uk, w_uv, w_out,
                            ln1_g, ln1_b, w_fc1, w_fc2, ln2_g, ln2_b)
        mod_lat = mods[l, 0:bsz].reshape(bsz, 6, D_MODEL)
        mod_ctx = jnp.broadcast_to(mods[l, bsz].reshape(1, 6, D_MODEL), (bsz, 6, D_MODEL))
        n_all = n_tok + n_ctx
        lat = _inproj(x, mod_lat, wl, tabs, rope=True, tm=INPROJ_TM, n_all=n_all)
        qa, ka, va, qb, qc = lat[0], lat[1], lat[2], lat[3], lat[6]
        cx = _inproj(xc, mod_ctx, wl, tabs_ctx, rope=False, tm=n_ctx, n_all=n_all,
                     merged=tuple(lat[o] for o in MERGED_OUTS))
        kb, vb, kc, vc = (cx[o] for o in MERGED_OUTS)
        ya = _window_attn(sink_a[l], qa, cx[1], cx[2], ka, va, tq=WINDOW_TQ, n_sub=WINDOW_TILES_PER_STEP)
        yb = _flash(qb, kb, vb, shared_kv=True, tq=FLASH_TQ, tk=FLASH_TK)
        yc = _flash(qc, kc, vc, shared_kv=False, tq=FLASH_TQ, tk=FLASH_TK)
        x = _mlp(x, ya, yb, yc, mod_lat, wl, tm=MLP_TM)
        if not last:
            yca, ycb, ycc = _ctx_attn(sink_a[l], cx)
            xc = _mlp(xc, yca, ycb, ycc, mod_ctx, wl, tm=n_ctx)
    return x
```

```python
import functools

import jax
import jax.numpy as jnp
from jax import lax
from jax.experimental import pallas as pl
from jax.experimental.pallas import tpu as pltpu

F32 = jnp.float32
BF16 = jnp.bfloat16

D_MODEL = 1024
DEPTH = 2
GRID_W = 64
HEAD_DIM = 64
WINDOW = 128
A_HEADS, A_KV_HEADS = 4, 2
B_HEADS, B_KV_HEADS = 4, 2
C_HEADS = 8
C_Q_RANK, C_KV_RANK = 256, 128
C_NOPE_DIM, C_ROPE_DIM, C_V_DIM = 64, 32, 64
D_FF = 4 * D_MODEL
ROPE_THETA = 10000.0
NORM_EPS = 1e-6
NEG_INF = -1e30
DEEPNORM_ALPHA = (2 * DEPTH) ** 0.25

OFF_AQ, OFF_AK, OFF_AV = 0, 256, 384
OFF_BQ, OFF_BK, OFF_BV = 512, 768, 896
OFF_CQ, OFF_CKV, OFF_CKR = 1024, 1280, 1408
LANES = 128
C_QK_PAD = 128
V_PAD = 128
LOG2E = 1.4426950408889634

VMEM_LIMIT = 56 * 1024 * 1024

INPROJ_TM = 512
FLASH_TQ, FLASH_TK = 1024, 2816
WINDOW_TQ, WINDOW_TILES_PER_STEP = 256, 4
MLP_TM, MLP_SUB, MLP_FF_CHUNK = 512, 256, 1024
MOD_TN = 1024


def _cparams(sem):
    return pltpu.CompilerParams(dimension_semantics=sem, vmem_limit_bytes=VMEM_LIMIT)


def _full(shape):
    n = len(shape)
    return pl.BlockSpec(shape, lambda *_: (0,) * n)


def _mod_kernel(cc_ref, w_ref, b_ref, o_ref):
    cc = cc_ref[...]
    s = cc / (1.0 + jnp.exp(-cc))
    o_ref[0] = jnp.dot(s, w_ref[0], preferred_element_type=F32,
                       precision=lax.Precision.HIGHEST) + b_ref[0]


def _modulation(cc, w_mod, b_mod):
    n_l = w_mod.shape[0]
    tn = MOD_TN
    return pl.pallas_call(
        _mod_kernel,
        grid=(n_l, 6 * D_MODEL // tn),
        in_specs=[pl.BlockSpec((8, D_MODEL), lambda l, j: (0, 0)),
                  pl.BlockSpec((1, D_MODEL, tn), lambda l, j: (l, 0, j)),
                  pl.BlockSpec((1, 1, tn), lambda l, j: (l, 0, j))],
        out_specs=pl.BlockSpec((1, 8, tn), lambda l, j: (l, 0, j)),
        out_shape=jax.ShapeDtypeStruct((n_l, 8, 6 * D_MODEL), F32),
        compiler_params=_cparams(("parallel", "parallel")),
        name="modulation",
    )(cc, w_mod, b_mod.reshape(n_l, 1, 6 * D_MODEL))


def _rope(x, cos, sin_signed, half):
    width = x.shape[-1]
    lane = lax.broadcasted_iota(jnp.int32, x.shape, 1)
    first = (lane % (2 * half)) < half
    nxt = pltpu.roll(x, width - half, axis=1)
    prv = pltpu.roll(x, half, axis=1)
    return x * cos + jnp.where(first, nxt, prv) * sin_signed


def _rope_halves(x, cos, sin_signed):
    return x * cos + pltpu.roll(x, LANES // 2, axis=1) * sin_signed


def _group_mean_sq(x, g_ref):
    x2 = x * x
    hi = x2.astype(BF16)
    lo = (x2 - hi.astype(F32)).astype(BF16)
    g = g_ref[...]
    return (jnp.dot(hi, g, preferred_element_type=F32) + jnp.dot(lo, g, preferred_element_type=F32))


def _row_rms(x, gain):
    ms = jnp.mean(x * x, axis=-1, keepdims=True)
    return x * lax.rsqrt(ms + NORM_EPS) * gain


N_INPROJ_IN = 16
MERGED_OUTS = (4, 5, 7, 8)


def _inproj_kernel(*refs, rope):
    (x_ref, mod_ref, w_in_ref, w_kr_ref, w_uq_ref, w_uk_ref, w_uv_ref, gq_ref, gk_ref, gmat_ref, mqn_ref, mkvn_ref,
     cos_a_ref, sin_a_ref, cos_q_ref, sin_q_ref) = refs[:N_INPROJ_IN]
    qa_ref, ka_ref, va_ref, qb_ref, kb_ref, vb_ref, qc_ref, kc_ref, vc_ref = refs[-9:]
    x = x_ref[0]
    shift, scale = mod_ref[0, 0:1, :], mod_ref[0, 1:2, :]
    h = (x * (1.0 + scale) + shift).astype(BF16)
    p = jnp.dot(h, w_in_ref[...], preferred_element_type=F32)

    def store_heads(ref, val, n_heads, width):
        for hh in range(n_heads):
            ref[0, hh] = val[:, hh * width:(hh + 1) * width].astype(ref.dtype)

    v_lanes = lax.broadcasted_iota(jnp.int32, (x.shape[0], V_PAD), 1) < HEAD_DIM

    def store_values(ref, val):
        ref[0, 0] = jnp.where(v_lanes, val, 1.0).astype(BF16)
        ref[0, 1] = jnp.where(v_lanes, pltpu.roll(val, HEAD_DIM, axis=1), 1.0).astype(BF16)

    qa = p[:, OFF_AQ:OFF_AQ + 256]
    ka = p[:, OFF_AK:OFF_AK + 128]
    if rope:
        qa = _rope(qa, cos_a_ref[...], sin_a_ref[...], HEAD_DIM // 2)
        ka = _rope(ka, cos_a_ref[:, 0:128], sin_a_ref[:, 0:128], HEAD_DIM // 2)
    store_heads(qa_ref, qa * (HEAD_DIM ** -0.5 * LOG2E), A_HEADS, HEAD_DIM)
    store_heads(ka_ref, ka, A_KV_HEADS, HEAD_DIM)
    store_values(va_ref, p[:, OFF_AV:OFF_AV + 128])

    qb = p[:, OFF_BQ:OFF_BQ + 256]
    kb = p[:, OFF_BK:OFF_BK + 128]
    qb = qb * lax.rsqrt(_group_mean_sq(qb, gmat_ref) + NORM_EPS) * gq_ref[...]
    kb = kb * lax.rsqrt(_group_mean_sq(kb, gmat_ref.at[0:128, 0:128]) + NORM_EPS) * gk_ref[...]
    if rope:
        qb = _rope(qb, cos_a_ref[...], sin_a_ref[...], HEAD_DIM // 2)
        kb = _rope(kb, cos_a_ref[:, 0:128], sin_a_ref[:, 0:128], HEAD_DIM // 2)
    store_heads(qb_ref, qb * (HEAD_DIM ** -0.5 * LOG2E), B_HEADS, HEAD_DIM)
    store_heads(kb_ref, kb, B_KV_HEADS, HEAD_DIM)
    store_values(vb_ref, p[:, OFF_BV:OFF_BV + 128])

    cq = _row_rms(p[:, OFF_CQ:OFF_CQ + C_Q_RANK], mqn_ref[...]).astype(BF16)
    ckv = _row_rms(p[:, OFF_CKV:OFF_CKV + C_KV_RANK], mkvn_ref[...]).astype(BF16)
    q = jnp.dot(cq, w_uq_ref[...], preferred_element_type=F32)
    kn = jnp.dot(ckv, w_uk_ref[...], preferred_element_type=F32)
    vc = jnp.dot(ckv, w_uv_ref[...], preferred_element_type=F32)
    kr = jnp.dot(h, w_kr_ref[...], preferred_element_type=F32)
    if rope:
        kr = _rope_halves(kr, cos_q_ref[...], sin_q_ref[...])
    c_scale = (C_NOPE_DIM + C_ROPE_DIM) ** -0.5 * LOG2E
    for hh in range(C_HEADS):
        qh = q[:, hh * C_QK_PAD:(hh + 1) * C_QK_PAD]
        if rope:
            qh = _rope_halves(qh, cos_q_ref[...], sin_q_ref[...])
        qc_ref[0, hh] = (qh * c_scale).astype(BF16)
        kc_ref[0, hh] = (kn[:, hh * C_QK_PAD:(hh + 1) * C_QK_PAD] + kr).astype(BF16)
        vc_ref[0, hh] = jnp.where(v_lanes, vc[:, hh * V_PAD:(hh + 1) * V_PAD], 1.0).astype(BF16)


def _inproj(x, mod, wl, tabs, *, rope, tm, n_all, merged=()):
    bsz, n_tok, _ = x.shape
    assert n_tok % tm == 0 and (not merged or (n_all - n_tok) % tm == 0)
    nt = n_tok // tm
    row_off = (n_all - n_tok) // tm if merged else 0
    out_dims = ((A_HEADS, HEAD_DIM), (A_KV_HEADS, HEAD_DIM), (A_KV_HEADS, V_PAD),
                (B_HEADS, HEAD_DIM), (B_KV_HEADS, HEAD_DIM), (B_KV_HEADS, V_PAD),
                (C_HEADS, C_QK_PAD), (C_HEADS, C_QK_PAD), (C_HEADS, V_PAD))
    tok = lambda width: pl.BlockSpec((tm, width), lambda i, b: (i, 0))

    def out_spec(o, n, width):
        off = row_off if o in MERGED_OUTS else 0
        return pl.BlockSpec((1, n, tm, width), lambda i, b: (b, 0, i + off, 0))

    def out_shape(o, n, width):
        return jax.ShapeDtypeStruct((bsz, n, n_all if o in MERGED_OUTS else n_tok, width), BF16)

    return pl.pallas_call(
        functools.partial(_inproj_kernel, rope=rope),
        grid=(nt, bsz),
        in_specs=[pl.BlockSpec((1, tm, D_MODEL), lambda i, b: (b, i, 0)),
                  pl.BlockSpec((1, 6, D_MODEL), lambda i, b: (b, 0, 0)),
                  _full((D_MODEL, OFF_CKR)), _full((D_MODEL, C_QK_PAD)), _full((C_Q_RANK, C_HEADS * C_QK_PAD)),
                  _full((C_KV_RANK, C_HEADS * C_QK_PAD)), _full((C_KV_RANK, C_HEADS * V_PAD)),
                  _full((1, 256)), _full((1, 128)), _full((256, 256)),
                  _full((1, C_Q_RANK)), _full((1, C_KV_RANK)),
                  tok(256), tok(256), tok(128), tok(128)]
                 + [pl.BlockSpec(memory_space=pl.ANY)] * len(merged),
        out_specs=[out_spec(o, *d) for o, d in enumerate(out_dims)],
        out_shape=[out_shape(o, *d) for o, d in enumerate(out_dims)],
        input_output_aliases={N_INPROJ_IN + j: o for j, o in enumerate(MERGED_OUTS)} if merged else {},
        compiler_params=_cparams(("parallel", "parallel")),
        name="inproj_rope" if rope else "inproj_ctx",
    )(x, mod, wl["w_in"], wl["w_kr"], wl["w_uq"], wl["w_uk"], wl["w_uv"], wl["gq"], wl["gk"], wl["gmat"],
      wl["mqn"], wl["mkvn"], *tabs, *merged)


def _qk(q, k):
    return lax.dot_general(q, k, (((1,), (1,)), ((), ())), preferred_element_type=F32)


def _online_step(q, k, v, m, acc):
    s = _qk(q, k)
    m_new = jnp.maximum(m, jnp.max(s, axis=-1, keepdims=True))
    p = jnp.exp2(s - m_new)
    acc = jnp.exp2(m - m_new) * acc + jnp.dot(p.astype(v.dtype), v, preferred_element_type=F32)
    return m_new, acc


def _softmax_attend(q, k, v, sink=None):
    s = _qk(q, k)
    m = jnp.max(s, axis=-1, keepdims=True)
    if sink is not None:
        m = jnp.maximum(m, sink)
    p = jnp.exp2(s - m)
    l = jnp.sum(p, axis=-1, keepdims=True)
    if sink is not None:
        l = l + jnp.exp2(sink - m)
    return jnp.dot(p.astype(v.dtype), v, preferred_element_type=F32) / l


def _flash_kernel(q_ref, k_ref, v_ref, o_ref, *, shared_kv, tk):
    tq = q_ref.shape[2]
    dv = o_ref.shape[2] // 2
    heads = ((q_ref[0, 0], 0), (q_ref[0, 1], 0 if shared_kv else 1))
    carry = [(jnp.full((tq, 1), NEG_INF, F32), jnp.zeros((tq, v_ref.shape[3]), F32)) for _ in heads]
    for off in range(0, k_ref.shape[2], tk):
        carry = [_online_step(q, k_ref[0, hk, off:off + tk, :], v_ref[0, hk, off:off + tk, :], *c)
                 for (q, hk), c in zip(heads, carry)]
    outs = [(acc / pltpu.roll(acc, dv, axis=1))[:, :dv] for _, acc in carry]
    o_ref[0] = jnp.concatenate(outs, axis=-1).astype(o_ref.dtype)


def _flash(q, k, v, *, shared_kv, tq, tk):
    bsz, n_heads, n_tok, dk = q.shape
    n_keys, dvp = v.shape[2], v.shape[3]
    dv = dvp // 2
    assert n_tok % tq == 0 and n_keys % tk == 0
    kvb = 1 if shared_kv else 2
    kv_spec = lambda width: pl.BlockSpec((1, kvb, n_keys, width), lambda b, hp, i: (b, hp, 0, 0))
    return pl.pallas_call(
        functools.partial(_flash_kernel, shared_kv=shared_kv, tk=tk),
        grid=(bsz, n_heads // 2, n_tok // tq),
        in_specs=[pl.BlockSpec((1, 2, tq, dk), lambda b, hp, i: (b, hp, i, 0)), kv_spec(dk), kv_spec(dvp)],
        out_specs=pl.BlockSpec((1, tq, 2 * dv), lambda b, hp, i: (b, i, hp)),
        out_shape=jax.ShapeDtypeStruct((bsz, n_tok, n_heads * dv), BF16),
        compiler_params=_cparams(("parallel", "parallel", "arbitrary")),
        name="flash_gqa" if shared_kv else "flash_mla",
    )(q, k, v)


def _window_kernel(sink_ref, q_ref, bias_ref, kc_ref, vc_ref, kl_ref, vl_ref, o_ref):
    tq, span = bias_ref.shape[1], bias_ref.shape[2]
    d, n_tok = q_ref.shape[3], kl_ref.shape[2]
    n_sub = q_ref.shape[2] // tq
    n_tiles = n_tok // tq
    row1 = lax.broadcasted_iota(jnp.int32, (2 * tq, 1), 0)
    for sub in range(n_sub):
        tile = pl.program_id(1) * n_sub + sub
        start = pl.multiple_of(jnp.clip(tile * tq - WINDOW, 0, n_tok - span), WINDOW)
        bias = bias_ref[jnp.where(tile == 0, 0, jnp.where(tile == n_tiles - 1, 2, 1))]
        rows = slice(sub * tq, (sub + 1) * tq)
        outs = []
        for kvh in range(kl_ref.shape[1]):
            q = q_ref[0, 2 * kvh:2 * kvh + 2, rows, :].reshape(2 * tq, d)
            s_loc = (_qk(q, kl_ref[0, kvh, pl.ds(start, span), :]).reshape(2, tq, span) + bias).reshape(2 * tq, span)
            s_ctx = _qk(q, kc_ref[0, kvh])
            sink = jnp.where(row1 < tq, sink_ref[2 * kvh], sink_ref[2 * kvh + 1]) * LOG2E
            m = jnp.maximum(jnp.maximum(jnp.max(s_loc, axis=-1, keepdims=True),
                                        jnp.max(s_ctx, axis=-1, keepdims=True)), sink)
            acc = (jnp.dot(jnp.exp2(s_loc - m).astype(BF16), vl_ref[0, kvh, pl.ds(start, span), :],
                           preferred_element_type=F32)
                   + jnp.dot(jnp.exp2(s_ctx - m).astype(BF16), vc_ref[0, kvh], preferred_element_type=F32))
            o = (acc / (pltpu.roll(acc, d, axis=1) + jnp.exp2(sink - m)))[:, :d]
            outs += [o[:tq], o[tq:]]
        o_ref[0, rows, :] = jnp.concatenate(outs, axis=-1).astype(o_ref.dtype)


def _window_attn(sink, q, k_ctx, v_ctx, k_lat, v_lat, *, tq, n_sub):
    bsz, n_heads, n_tok, d = q.shape
    tqs = tq * n_sub
    span = tq + 2 * WINDOW
    assert n_tok % tqs == 0 and n_tok // tq >= 2 and span <= n_tok and tq % WINDOW == 0
    rel = jnp.arange(span)[None, None, :] - jnp.arange(tq)[None, :, None] + jnp.array([0, -WINDOW, tq - span])[:, None, None]
    bias = jnp.where(jnp.abs(rel) <= WINDOW, 0.0, NEG_INF).astype(F32)
    kv_spec = lambda a: pl.BlockSpec((1,) + a.shape[1:], lambda b, i: (b, 0, 0, 0))
    return pl.pallas_call(
        _window_kernel,
        grid=(bsz, n_tok // tqs),
        in_specs=[pl.BlockSpec(memory_space=pltpu.SMEM),
                  pl.BlockSpec((1, n_heads, tqs, d), lambda b, i: (b, 0, i, 0)),
                  pl.BlockSpec((3, tq, span), lambda b, i: (0, 0, 0)),
                  kv_spec(k_ctx), kv_spec(v_ctx), kv_spec(k_lat), kv_spec(v_lat)],
        out_specs=pl.BlockSpec((1, tqs, n_heads * d), lambda b, i: (b, i, 0)),
        out_shape=jax.ShapeDtypeStruct((bsz, n_tok, n_heads * d), BF16),
        compiler_params=_cparams(("parallel", "arbitrary")),
        name="window_gqa",
    )(sink, q, bias, k_ctx, v_ctx, k_lat, v_lat)


def _ctx_attn_kernel(sink_ref, qa_ref, ka_ref, va_ref, qb_ref, kb_ref, vb_ref, qc_ref, kc_ref, vc_ref,
                     ya_ref, yb_ref, yc_ref):
    n = qa_ref.shape[2]
    row1 = lax.broadcasted_iota(jnp.int32, (2 * n, 1), 0)

    def gqa(q_ref, k_ref, v_ref, y_ref, with_sink):
        outs = []
        for kvh in range(k_ref.shape[1]):
            q = q_ref[0, 2 * kvh:2 * kvh + 2].reshape(2 * n, q_ref.shape[3])
            sink = (jnp.where(row1 < n, sink_ref[2 * kvh], sink_ref[2 * kvh + 1]) * LOG2E
                    if with_sink else None)
            o = _softmax_attend(q, k_ref[0, kvh], v_ref[0, kvh, :, 0:HEAD_DIM], sink)
            outs += [o[:n], o[n:]]
        y_ref[0] = jnp.concatenate(outs, axis=-1).astype(y_ref.dtype)

    gqa(qa_ref, ka_ref, va_ref, ya_ref, True)
    gqa(qb_ref, kb_ref, vb_ref, yb_ref, False)
    outs = [_softmax_attend(qc_ref[0, hh], kc_ref[0, hh], vc_ref[0, hh, :, 0:C_V_DIM]) for hh in range(C_HEADS)]
    yc_ref[0] = jnp.concatenate(outs, axis=-1).astype(yc_ref.dtype)


def _ctx_attn(sink, cx):
    bsz, _, n, _ = cx[0].shape
    spec = lambda a: pl.BlockSpec((1, a.shape[1], n, a.shape[3]), lambda b: (b, 0, a.shape[2] // n - 1, 0))
    widths = (A_HEADS * HEAD_DIM, B_HEADS * HEAD_DIM, C_HEADS * C_V_DIM)
    return pl.pallas_call(
        _ctx_attn_kernel,
        grid=(bsz,),
        in_specs=[pl.BlockSpec(memory_space=pltpu.SMEM)] + [spec(a) for a in cx],
        out_specs=[pl.BlockSpec((1, n, w), lambda b: (b, 0, 0)) for w in widths],
        out_shape=[jax.ShapeDtypeStruct((bsz, n, w), BF16) for w in widths],
        compiler_params=_cparams(("parallel",)),
        name="ctx_attn",
    )(sink, *cx)


def _layer_norm(v, g, b):
    mu = jnp.mean(v, axis=-1, keepdims=True)
    d = v - mu
    var = jnp.mean(d * d, axis=-1, keepdims=True)
    return d * lax.rsqrt(var + NORM_EPS) * g + b


def _mlp_kernel(x_ref, ya_ref, yb_ref, yc_ref, mod_ref, w_out_ref, ln1_ref, w1_ref, w2_ref, ln2_ref,
                o_ref, *, ff_chunk, sub):
    g1, sh2, sc2, g2 = (mod_ref[0, r:r + 1, :] for r in (2, 3, 4, 5))
    tiles = [slice(r0, r0 + sub) for r0 in range(0, x_ref.shape[1], sub)]
    x1s, hs, accs = [], [], []
    for rows in tiles:
        y = (jnp.dot(ya_ref[0, rows, :], w_out_ref[0:256, :], preferred_element_type=F32)
             + jnp.dot(yb_ref[0, rows, :], w_out_ref[256:512, :], preferred_element_type=F32)
             + jnp.dot(yc_ref[0, rows, :], w_out_ref[512:1024, :], preferred_element_type=F32))
        x1s.append(_layer_norm(DEEPNORM_ALPHA * x_ref[0, rows, :] + g1 * y, ln1_ref[0:1, :], ln1_ref[1:2, :]))
        hs.append((x1s[-1] * (1.0 + sc2) + sh2).astype(BF16))
    for h in hs:
        acc = jnp.zeros((sub, D_MODEL), F32)
        for c0 in range(0, D_FF, ff_chunk):
            a = jnp.maximum(jnp.dot(h, w1_ref[:, c0:c0 + ff_chunk], preferred_element_type=F32), 0.0)
            acc = acc + jnp.dot((a * a).astype(BF16), w2_ref[c0:c0 + ff_chunk, :], preferred_element_type=F32)
        accs.append(acc)
    for rows, x1, acc in zip(tiles, x1s, accs):
        o_ref[0, rows, :] = _layer_norm(DEEPNORM_ALPHA * x1 + g2 * acc, ln2_ref[0:1, :], ln2_ref[1:2, :])


def _mlp(x, ya, yb, yc, mod, wl, *, tm):
    bsz, n_tok, _ = x.shape
    tok = lambda width: pl.BlockSpec((1, tm, width), lambda b, i: (b, i, 0))
    const = lambda shape: pl.BlockSpec(shape, lambda b, i: (0, 0), pipeline_mode=pl.Buffered(1))
    return pl.pallas_call(
        functools.partial(_mlp_kernel, ff_chunk=MLP_FF_CHUNK, sub=min(tm, MLP_SUB)),
        grid=(bsz, n_tok // tm),
        in_specs=[tok(D_MODEL), tok(256), tok(256), tok(512),
                  pl.BlockSpec((1, 6, D_MODEL), lambda b, i: (b, 0, 0)),
                  const((D_MODEL, D_MODEL)), const((2, D_MODEL)),
                  const((D_MODEL, D_FF)), const((D_FF, D_MODEL)), const((2, D_MODEL))],
        out_specs=tok(D_MODEL),
        out_shape=jax.ShapeDtypeStruct(x.shape, F32),
        compiler_params=_cparams(("parallel", "parallel")),
        name="outproj_mlp",
    )(x, ya, yb, yc, mod, wl["w_out"], wl["ln1"], wl["w_fc1"], wl["w_fc2"], wl["ln2"])


def _rope_tables(n_tok):
    n_rows = n_tok // GRID_W

    def parts(rot_dim):
        n_freq = rot_dim // 4
        inv = ROPE_THETA ** (-jnp.arange(n_freq, dtype=F32) / n_freq)
        ang_r = jnp.arange(n_rows, dtype=F32)[:, None] * inv
        ang_c = jnp.arange(GRID_W, dtype=F32)[:, None] * inv
        z_r, z_c = jnp.zeros_like(ang_r), jnp.zeros_like(ang_c)
        split = lambda f: (jnp.concatenate([f(ang_r), z_r], axis=-1), jnp.concatenate([z_c, f(ang_c)], axis=-1))
        return split(jnp.cos), split(jnp.sin)

    def expand(r_part, c_part):
        return (r_part[:, None, :] + c_part[None, :, :]).reshape(n_tok, r_part.shape[-1])

    lay_a = lambda v, sign: jnp.tile(jnp.concatenate([sign * v, v], axis=-1), (1, A_HEADS))
    lay_q = lambda v, sign, nope: _mla_lanes(jnp.full(v.shape[:-1] + (C_NOPE_DIM,), nope, F32),
                                             jnp.concatenate([sign * v, v], axis=-1))
    (c64r, c64c), (s64r, s64c) = parts(HEAD_DIM)
    (c32r, c32c), (s32r, s32c) = parts(C_ROPE_DIM)
    cos_a = expand(lay_a(c64r, 1.0), lay_a(c64c, 1.0))
    sin_a = expand(lay_a(s64r, -1.0), lay_a(s64c, -1.0))
    cos_q = expand(lay_q(c32r, 1.0, 0.0), lay_q(c32c, 1.0, 1.0))
    sin_q = expand(lay_q(s32r, -1.0, 0.0), lay_q(s32c, -1.0, 0.0))
    return cos_a, sin_a, cos_q, sin_q


def _mla_lanes(nope, rope):
    half = C_ROPE_DIM // 2
    split = LANES // 2 - half
    pad = jnp.zeros(nope.shape[:-1] + (C_QK_PAD - C_NOPE_DIM - C_ROPE_DIM,), nope.dtype)
    return jnp.concatenate([rope[..., :half], nope[..., :split], rope[..., half:], nope[..., split:], pad], axis=-1)


def _layer_weights(l, w_in, q_norm_b, k_norm_b, mla_q_norm, mla_kv_norm, w_uq, w_uk, w_uv, w_out,
                   ln1_g, ln1_b, w_fc1, w_fc2, ln2_g, ln2_b):
    uq = w_uq[l].reshape(C_Q_RANK, C_HEADS, C_NOPE_DIM + C_ROPE_DIM)
    uq = _mla_lanes(uq[..., :C_NOPE_DIM], uq[..., C_NOPE_DIM:])
    uk = w_uk[l].reshape(C_KV_RANK, C_HEADS, C_NOPE_DIM)
    uk = _mla_lanes(uk, jnp.zeros((C_KV_RANK, C_HEADS, C_ROPE_DIM), F32))
    w_kr = _mla_lanes(jnp.zeros((D_MODEL, C_NOPE_DIM), F32), w_in[l][:, OFF_CKR:OFF_CKR + C_ROPE_DIM])
    uv = w_uv[l].reshape(C_KV_RANK, C_HEADS, C_V_DIM)
    uv = jnp.pad(uv, ((0, 0), (0, 0), (0, V_PAD - C_V_DIM)))
    lane = jnp.arange(256)
    gmat = jnp.where((lane[:, None] // HEAD_DIM) == (lane[None, :] // HEAD_DIM), 1.0 / HEAD_DIM, 0.0)
    return {
        "w_in": w_in[l][:, :OFF_CKR].astype(BF16),
        "w_kr": w_kr.astype(BF16),
        "w_uq": uq.reshape(C_Q_RANK, C_HEADS * C_QK_PAD).astype(BF16),
        "w_uk": uk.reshape(C_KV_RANK, C_HEADS * C_QK_PAD).astype(BF16),
        "w_uv": uv.reshape(C_KV_RANK, C_HEADS * V_PAD).astype(BF16),
        "gq": jnp.tile(q_norm_b[l], B_HEADS)[None, :],
        "gk": jnp.tile(k_norm_b[l], B_KV_HEADS)[None, :],
        "gmat": gmat.astype(BF16),
        "mqn": mla_q_norm[l][None, :],
        "mkvn": mla_kv_norm[l][None, :],
        "w_out": w_out[l].astype(BF16),
        "ln1": jnp.stack([ln1_g[l], ln1_b[l]]),
        "w_fc1": w_fc1[l].astype(BF16),
        "w_fc2": w_fc2[l].astype(BF16),
        "ln2": jnp.stack([ln2_g[l], ln2_b[l]]),
    }


def kernel(x, c, ctx, c_ctx, w_mod, b_mod, w_in, sink_a, q_norm_b, k_norm_b, mla_q_norm, mla_kv_norm,
           w_uq, w_uk, w_uv, w_out, ln1_g, ln1_b, w_fc1, w_fc2, ln2_g, ln2_b):
    bsz, n_tok, _ = x.shape
    n_ctx = ctx.shape[1]
    cc = jnp.concatenate([c, c_ctx[None, :], jnp.zeros((8 - bsz - 1, D_MODEL), F32)], axis=0)
    mods = _modulation(cc, w_mod, b_mod)
    tabs = _rope_tables(n_tok)
    tabs_ctx = tuple(t[:n_ctx] for t in tabs)
    xc = ctx
    for l in range(DEPTH):
        last = l == DEPTH - 1
        wl = _layer_weights(l, w_in, q_norm_b, k_norm_b, mla_q_norm, mla_kv_norm, w_uq, w_uk, w_uv, w_out,
                            ln1_g, ln1_b, w_fc1, w_fc2, ln2_g, ln2_b)
        mod_lat = mods[l, 0:bsz].reshape(bsz, 6, D_MODEL)
        mod_ctx = jnp.broadcast_to(mods[l, bsz].reshape(1, 6, D_MODEL), (bsz, 6, D_MODEL))
        n_all = n_tok + n_ctx
        lat = _inproj(x, mod_lat, wl, tabs, rope=True, tm=INPROJ_TM, n_all=n_all)
        qa, ka, va, qb, qc = lat[0], lat[1], lat[2], lat[3], lat[6]
        cx = _inproj(xc, mod_ctx, wl, tabs_ctx, rope=False, tm=n_ctx, n_all=n_all,
                     merged=tuple(lat[o] for o in MERGED_OUTS))
        kb, vb, kc, vc = (cx[o] for o in MERGED_OUTS)
        ya = _window_attn(sink_a[l], qa, cx[1], cx[2], ka, va, tq=WINDOW_TQ, n_sub=WINDOW_TILES_PER_STEP)
        yb = _flash(qb, kb, vb, shared_kv=True, tq=FLASH_TQ, tk=FLASH_TK)
        yc = _flash(qc, kc, vc, shared_kv=False, tq=FLASH_TQ, tk=FLASH_TK)
        x = _mlp(x, ya, yb, yc, mod_lat, wl, tm=MLP_TM)
        if not last:
            yca, ycb, ycc = _ctx_attn(sink_a[l], cx)
            xc = _mlp(xc, yca, ycb, ycc, mod_ctx, wl, tm=n_ctx)
    return x
```

```python
import functools

import jax
import jax.numpy as jnp
from jax import lax
from jax.experimental import pallas as pl
from jax.experimental.pallas import tpu as pltpu

F32 = jnp.float32
BF16 = jnp.bfloat16

D_MODEL = 1024
DEPTH = 2
GRID_W = 64
HEAD_DIM = 64
WINDOW = 128
A_HEADS, A_KV_HEADS = 4, 2
B_HEADS, B_KV_HEADS = 4, 2
C_HEADS = 8
C_Q_RANK, C_KV_RANK = 256, 128
C_NOPE_DIM, C_ROPE_DIM, C_V_DIM = 64, 32, 64
D_FF = 4 * D_MODEL
ROPE_THETA = 10000.0
NORM_EPS = 1e-6
NEG_INF = -1e30
DEEPNORM_ALPHA = (2 * DEPTH) ** 0.25

OFF_AQ, OFF_AK, OFF_AV = 0, 256, 384
OFF_BQ, OFF_BK, OFF_BV = 512, 768, 896
OFF_CQ, OFF_CKV, OFF_CKR = 1024, 1280, 1408
LANES = 128
C_QK_PAD = 128
V_PAD = 128
LOG2E = 1.4426950408889634

VMEM_LIMIT = 56 * 1024 * 1024

INPROJ_TM = 512
FLASH_TQ, FLASH_TK = 1024, 2816
WINDOW_TQ, WINDOW_TILES_PER_STEP = 256, 4
MLP_TM, MLP_SUB, MLP_FF_CHUNK = 512, 256, 1024
MOD_TN = 1024


def _cparams(sem):
    return pltpu.CompilerParams(dimension_semantics=sem, vmem_limit_bytes=VMEM_LIMIT)


def _full(shape):
    n = len(shape)
    return pl.BlockSpec(shape, lambda *_: (0,) * n)


def _mod_kernel(cc_ref, w_ref, b_ref, o_ref):
    cc = cc_ref[...]
    s = cc / (1.0 + jnp.exp(-cc))
    o_ref[0] = jnp.dot(s, w_ref[0], preferred_element_type=F32,
                       precision=lax.Precision.HIGHEST) + b_ref[0]


def _modulation(cc, w_mod, b_mod):
    n_l = w_mod.shape[0]
    tn = MOD_TN
    return pl.pallas_call(
        _mod_kernel,
        grid=(n_l, 6 * D_MODEL // tn),
        in_specs=[pl.BlockSpec((8, D_MODEL), lambda l, j: (0, 0)),
                  pl.BlockSpec((1, D_MODEL, tn), lambda l, j: (l, 0, j)),
                  pl.BlockSpec((1, 1, tn), lambda l, j: (l, 0, j))],
        out_specs=pl.BlockSpec((1, 8, tn), lambda l, j: (l, 0, j)),
        out_shape=jax.ShapeDtypeStruct((n_l, 8, 6 * D_MODEL), F32),
        compiler_params=_cparams(("parallel", "parallel")),
        name="modulation",
    )(cc, w_mod, b_mod.reshape(n_l, 1, 6 * D_MODEL))


def _rope(x, cos, sin_signed, half):
    width = x.shape[-1]
    lane = lax.broadcasted_iota(jnp.int32, x.shape, 1)
    first = (lane % (2 * half)) < half
    nxt = pltpu.roll(x, width - half, axis=1)
    prv = pltpu.roll(x, half, axis=1)
    return x * cos + jnp.where(first, nxt, prv) * sin_signed


def _rope_halves(x, cos, sin_signed):
    return x * cos + pltpu.roll(x, LANES // 2, axis=1) * sin_signed


def _group_mean_sq(x, g_ref):
    x2 = x * x
    hi = x2.astype(BF16)
    lo = (x2 - hi.astype(F32)).astype(BF16)
    g = g_ref[...]
    return (jnp.dot(hi, g, preferred_element_type=F32) + jnp.dot(lo, g, preferred_element_type=F32))


def _row_rms(x, gain):
    ms = jnp.mean(x * x, axis=-1, keepdims=True)
    return x * lax.rsqrt(ms + NORM_EPS) * gain


def _inproj_kernel(x_ref, xc_ref, modl_ref, modc_ref, w_in_ref, w_kr_ref, w_uq_ref, w_uk_ref, w_uv_ref,
                   gq_ref, gk_ref, gmat_ref, mqn_ref, mkvn_ref, cos_a_ref, sin_a_ref, cos_q_ref, sin_q_ref,
                   qa_ref, ka_ref, va_ref, qb_ref, kb_ref, vb_ref, qc_ref, kc_ref, vc_ref):
    is_ctx = pl.program_id(0) == pl.num_programs(0) - 1
    x = jnp.where(is_ctx, xc_ref[0], x_ref[0])
    shift = jnp.where(is_ctx, modc_ref[0, 0:1, :], modl_ref[0, 0:1, :])
    scale = jnp.where(is_ctx, modc_ref[0, 1:2, :], modl_ref[0, 1:2, :])
    cos_a, sin_a = jnp.where(is_ctx, 1.0, cos_a_ref[...]), jnp.where(is_ctx, 0.0, sin_a_ref[...])
    cos_q, sin_q = jnp.where(is_ctx, 1.0, cos_q_ref[...]), jnp.where(is_ctx, 0.0, sin_q_ref[...])
    h = (x * (1.0 + scale) + shift).astype(BF16)
    p = jnp.dot(h, w_in_ref[...], preferred_element_type=F32)

    def store_heads(ref, val, n_heads, width):
        for hh in range(n_heads):
            ref[0, hh] = val[:, hh * width:(hh + 1) * width].astype(ref.dtype)

    v_lanes = lax.broadcasted_iota(jnp.int32, (x.shape[0], V_PAD), 1) < HEAD_DIM

    def store_values(ref, val):
        ref[0, 0] = jnp.where(v_lanes, val, 1.0).astype(BF16)
        ref[0, 1] = jnp.where(v_lanes, pltpu.roll(val, HEAD_DIM, axis=1), 1.0).astype(BF16)

    qa = p[:, OFF_AQ:OFF_AQ + 256]
    ka = p[:, OFF_AK:OFF_AK + 128]
    qa = _rope(qa, cos_a, sin_a, HEAD_DIM // 2)
    ka = _rope(ka, cos_a[:, 0:128], sin_a[:, 0:128], HEAD_DIM // 2)
    store_heads(qa_ref, qa * (HEAD_DIM ** -0.5 * LOG2E), A_HEADS, HEAD_DIM)
    store_heads(ka_ref, ka, A_KV_HEADS, HEAD_DIM)
    store_values(va_ref, p[:, OFF_AV:OFF_AV + 128])

    qb = p[:, OFF_BQ:OFF_BQ + 256]
    kb = p[:, OFF_BK:OFF_BK + 128]
    qb = qb * lax.rsqrt(_group_mean_sq(qb, gmat_ref) + NORM_EPS) * gq_ref[...]
    kb = kb * lax.rsqrt(_group_mean_sq(kb, gmat_ref.at[0:128, 0:128]) + NORM_EPS) * gk_ref[...]
    qb = _rope(qb, cos_a, sin_a, HEAD_DIM // 2)
    kb = _rope(kb, cos_a[:, 0:128], sin_a[:, 0:128], HEAD_DIM // 2)
    store_heads(qb_ref, qb * (HEAD_DIM ** -0.5 * LOG2E), B_HEADS, HEAD_DIM)
    store_heads(kb_ref, kb, B_KV_HEADS, HEAD_DIM)
    store_values(vb_ref, p[:, OFF_BV:OFF_BV + 128])

    cq = _row_rms(p[:, OFF_CQ:OFF_CQ + C_Q_RANK], mqn_ref[...]).astype(BF16)
    ckv = _row_rms(p[:, OFF_CKV:OFF_CKV + C_KV_RANK], mkvn_ref[...]).astype(BF16)
    q = jnp.dot(cq, w_uq_ref[...], preferred_element_type=F32)
    kn = jnp.dot(ckv, w_uk_ref[...], preferred_element_type=F32)
    vc = jnp.dot(ckv, w_uv_ref[...], preferred_element_type=F32)
    kr = jnp.dot(h, w_kr_ref[...], preferred_element_type=F32)
    kr = _rope_halves(kr, cos_q, sin_q)
    c_scale = (C_NOPE_DIM + C_ROPE_DIM) ** -0.5 * LOG2E
    for hh in range(C_HEADS):
        qh = _rope_halves(q[:, hh * C_QK_PAD:(hh + 1) * C_QK_PAD], cos_q, sin_q)
        qc_ref[0, hh] = (qh * c_scale).astype(BF16)
        kc_ref[0, hh] = (kn[:, hh * C_QK_PAD:(hh + 1) * C_QK_PAD] + kr).astype(BF16)
        vc_ref[0, hh] = jnp.where(v_lanes, vc[:, hh * V_PAD:(hh + 1) * V_PAD], 1.0).astype(BF16)


def _inproj(x, xc, mod_lat, mod_ctx, wl, tabs, *, tm):
    bsz, n_tok, _ = x.shape
    n_ctx = xc.shape[1]
    assert n_tok % tm == 0 and n_ctx <= tm
    nt = n_tok // tm
    xc_pad = jnp.pad(xc, ((0, 0), (0, tm - n_ctx), (0, 0)))
    out_dims = ((A_HEADS, HEAD_DIM), (A_KV_HEADS, HEAD_DIM), (A_KV_HEADS, V_PAD),
                (B_HEADS, HEAD_DIM), (B_KV_HEADS, HEAD_DIM), (B_KV_HEADS, V_PAD),
                (C_HEADS, C_QK_PAD), (C_HEADS, C_QK_PAD), (C_HEADS, V_PAD))
    lat = lambda i: jnp.minimum(i, nt - 1)
    tok = lambda width: pl.BlockSpec((tm, width), lambda i, b: (lat(i), 0))
    return pl.pallas_call(
        _inproj_kernel,
        grid=(nt + 1, bsz),
        in_specs=[pl.BlockSpec((1, tm, D_MODEL), lambda i, b: (b, lat(i), 0)),
                  pl.BlockSpec((1, tm, D_MODEL), lambda i, b: (jnp.where(i == nt, b, 0), 0, 0)),
                  pl.BlockSpec((1, 6, D_MODEL), lambda i, b: (b, 0, 0)), _full((1, 6, D_MODEL)),
                  _full((D_MODEL, OFF_CKR)), _full((D_MODEL, C_QK_PAD)), _full((C_Q_RANK, C_HEADS * C_QK_PAD)),
                  _full((C_KV_RANK, C_HEADS * C_QK_PAD)), _full((C_KV_RANK, C_HEADS * V_PAD)),
                  _full((1, 256)), _full((1, 128)), _full((256, 256)),
                  _full((1, C_Q_RANK)), _full((1, C_KV_RANK)),
                  tok(256), tok(256), tok(128), tok(128)],
        out_specs=[pl.BlockSpec((1, n, tm, width), lambda i, b: (b, 0, i, 0)) for n, width in out_dims],
        out_shape=[jax.ShapeDtypeStruct((bsz, n, (nt + 1) * tm, width), BF16) for n, width in out_dims],
        compiler_params=_cparams(("arbitrary", "parallel")),
        name="inproj",
    )(x, xc_pad, mod_lat, mod_ctx, wl["w_in"], wl["w_kr"], wl["w_uq"], wl["w_uk"], wl["w_uv"], wl["gq"], wl["gk"],
      wl["gmat"], wl["mqn"], wl["mkvn"], *tabs)


def _qk(q, k):
    return lax.dot_general(q, k, (((1,), (1,)), ((), ())), preferred_element_type=F32)


def _online_step(q, k, v, m, acc):
    s = _qk(q, k)
    m_new = jnp.maximum(m, jnp.max(s, axis=-1, keepdims=True))
    p = jnp.exp2(s - m_new)
    acc = jnp.exp2(m - m_new) * acc + jnp.dot(p.astype(v.dtype), v, preferred_element_type=F32)
    return m_new, acc


def _softmax_attend(q, k, v, sink=None):
    s = _qk(q, k)
    m = jnp.max(s, axis=-1, keepdims=True)
    if sink is not None:
        m = jnp.maximum(m, sink)
    p = jnp.exp2(s - m)
    l = jnp.sum(p, axis=-1, keepdims=True)
    if sink is not None:
        l = l + jnp.exp2(sink - m)
    return jnp.dot(p.astype(v.dtype), v, preferred_element_type=F32) / l


def _flash_kernel(q_ref, k_ref, v_ref, o_ref, *, shared_kv, tk):
    tq = q_ref.shape[2]
    dv = o_ref.shape[2] // 2
    heads = ((q_ref[0, 0], 0), (q_ref[0, 1], 0 if shared_kv else 1))
    carry = [(jnp.full((tq, 1), NEG_INF, F32), jnp.zeros((tq, v_ref.shape[3]), F32)) for _ in heads]
    for off in range(0, k_ref.shape[2], tk):
        carry = [_online_step(q, k_ref[0, hk, off:off + tk, :], v_ref[0, hk, off:off + tk, :], *c)
                 for (q, hk), c in zip(heads, carry)]
    outs = [(acc / pltpu.roll(acc, dv, axis=1))[:, :dv] for _, acc in carry]
    o_ref[0] = jnp.concatenate(outs, axis=-1).astype(o_ref.dtype)


def _flash(q, k, v, *, n_tok, n_keys, shared_kv, tq, tk):
    bsz, n_heads, _, dk = q.shape
    dvp = v.shape[3]
    dv = dvp // 2
    assert n_tok % tq == 0 and n_keys % tk == 0
    kvb = 1 if shared_kv else 2
    kv_spec = lambda width: pl.BlockSpec((1, kvb, n_keys, width), lambda b, hp, i: (b, hp, 0, 0))
    return pl.pallas_call(
        functools.partial(_flash_kernel, shared_kv=shared_kv, tk=tk),
        grid=(bsz, n_heads // 2, n_tok // tq),
        in_specs=[pl.BlockSpec((1, 2, tq, dk), lambda b, hp, i: (b, hp, i, 0)), kv_spec(dk), kv_spec(dvp)],
        out_specs=pl.BlockSpec((1, tq, 2 * dv), lambda b, hp, i: (b, i, hp)),
        out_shape=jax.ShapeDtypeStruct((bsz, n_tok, n_heads * dv), BF16),
        compiler_params=_cparams(("parallel", "parallel", "arbitrary")),
        name="flash_gqa" if shared_kv else "flash_mla",
    )(q, k, v)


def _window_kernel(sink_ref, q_ref, bias_ref, kc_ref, vc_ref, kl_ref, vl_ref, o_ref):
    tq, span = bias_ref.shape[1], bias_ref.shape[2]
    d, n_tok = q_ref.shape[3], kl_ref.shape[2]
    n_sub = q_ref.shape[2] // tq
    n_tiles = n_tok // tq
    row1 = lax.broadcasted_iota(jnp.int32, (2 * tq, 1), 0)
    for sub in range(n_sub):
        tile = pl.program_id(1) * n_sub + sub
        start = pl.multiple_of(jnp.clip(tile * tq - WINDOW, 0, n_tok - span), WINDOW)
        bias = bias_ref[jnp.where(tile == 0, 0, jnp.where(tile == n_tiles - 1, 2, 1))]
        rows = slice(sub * tq, (sub + 1) * tq)
        outs = []
        for kvh in range(kl_ref.shape[1]):
            q = q_ref[0, 2 * kvh:2 * kvh + 2, rows, :].reshape(2 * tq, d)
            s_loc = (_qk(q, kl_ref[0, kvh, pl.ds(start, span), :]).reshape(2, tq, span) + bias).reshape(2 * tq, span)
            s_ctx = _qk(q, kc_ref[0, kvh])
            sink = jnp.where(row1 < tq, sink_ref[2 * kvh], sink_ref[2 * kvh + 1]) * LOG2E
            m = jnp.maximum(jnp.maximum(jnp.max(s_loc, axis=-1, keepdims=True),
                                        jnp.max(s_ctx, axis=-1, keepdims=True)), sink)
            acc = (jnp.dot(jnp.exp2(s_loc - m).astype(BF16), vl_ref[0, kvh, pl.ds(start, span), :],
                           preferred_element_type=F32)
                   + jnp.dot(jnp.exp2(s_ctx - m).astype(BF16), vc_ref[0, kvh], preferred_element_type=F32))
            o = (acc / (pltpu.roll(acc, d, axis=1) + jnp.exp2(sink - m)))[:, :d]
            outs += [o[:tq], o[tq:]]
        o_ref[0, rows, :] = jnp.concatenate(outs, axis=-1).astype(o_ref.dtype)


def _window_attn(sink, q, k, v, *, n_tok, n_ctx, tq, n_sub):
    bsz, n_heads, _, d = q.shape
    tqs = tq * n_sub
    span = tq + 2 * WINDOW
    assert n_tok % tqs == 0 and n_tok // tq >= 2 and span <= n_tok and tq % WINDOW == 0 and n_tok % n_ctx == 0
    rel = jnp.arange(span)[None, None, :] - jnp.arange(tq)[None, :, None] + jnp.array([0, -WINDOW, tq - span])[:, None, None]
    bias = jnp.where(jnp.abs(rel) <= WINDOW, 0.0, NEG_INF).astype(F32)
    ctx_spec = lambda a: pl.BlockSpec((1, a.shape[1], n_ctx, a.shape[3]), lambda b, i: (b, 0, n_tok // n_ctx, 0))
    lat_spec = lambda a: pl.BlockSpec((1, a.shape[1], n_tok, a.shape[3]), lambda b, i: (b, 0, 0, 0))
    return pl.pallas_call(
        _window_kernel,
        grid=(bsz, n_tok // tqs),
        in_specs=[pl.BlockSpec(memory_space=pltpu.SMEM),
                  pl.BlockSpec((1, n_heads, tqs, d), lambda b, i: (b, 0, i, 0)),
                  pl.BlockSpec((3, tq, span), lambda b, i: (0, 0, 0)),
                  ctx_spec(k), ctx_spec(v), lat_spec(k), lat_spec(v)],
        out_specs=pl.BlockSpec((1, tqs, n_heads * d), lambda b, i: (b, i, 0)),
        out_shape=jax.ShapeDtypeStruct((bsz, n_tok, n_heads * d), BF16),
        compiler_params=_cparams(("parallel", "arbitrary")),
        name="window_gqa",
    )(sink, q, bias, k, v, k, v)


def _ctx_attn_kernel(sink_ref, qa_ref, ka_ref, va_ref, qb_ref, kb_ref, vb_ref, qc_ref, kc_ref, vc_ref,
                     ya_ref, yb_ref, yc_ref):
    n = qa_ref.shape[2]
    row1 = lax.broadcasted_iota(jnp.int32, (2 * n, 1), 0)

    def gqa(q_ref, k_ref, v_ref, y_ref, with_sink):
        outs = []
        for kvh in range(k_ref.shape[1]):
            q = q_ref[0, 2 * kvh:2 * kvh + 2].reshape(2 * n, q_ref.shape[3])
            sink = (jnp.where(row1 < n, sink_ref[2 * kvh], sink_ref[2 * kvh + 1]) * LOG2E
                    if with_sink else None)
            o = _softmax_attend(q, k_ref[0, kvh], v_ref[0, kvh, :, 0:HEAD_DIM], sink)
            outs += [o[:n], o[n:]]
        y_ref[0] = jnp.concatenate(outs, axis=-1).astype(y_ref.dtype)

    gqa(qa_ref, ka_ref, va_ref, ya_ref, True)
    gqa(qb_ref, kb_ref, vb_ref, yb_ref, False)
    outs = [_softmax_attend(qc_ref[0, hh], kc_ref[0, hh], vc_ref[0, hh, :, 0:C_V_DIM]) for hh in range(C_HEADS)]
    yc_ref[0] = jnp.concatenate(outs, axis=-1).astype(yc_ref.dtype)


def _ctx_attn(sink, cx, *, n_tok, n_ctx):
    bsz, n = cx[0].shape[0], n_ctx
    assert n_tok % n_ctx == 0
    spec = lambda a: pl.BlockSpec((1, a.shape[1], n, a.shape[3]), lambda b: (b, 0, n_tok // n, 0))
    widths = (A_HEADS * HEAD_DIM, B_HEADS * HEAD_DIM, C_HEADS * C_V_DIM)
    return pl.pallas_call(
        _ctx_attn_kernel,
        grid=(bsz,),
        in_specs=[pl.BlockSpec(memory_space=pltpu.SMEM)] + [spec(a) for a in cx],
        out_specs=[pl.BlockSpec((1, n, w), lambda b: (b, 0, 0)) for w in widths],
        out_shape=[jax.ShapeDtypeStruct((bsz, n, w), BF16) for w in widths],
        compiler_params=_cparams(("parallel",)),
        name="ctx_attn",
    )(sink, *cx)


def _layer_norm(v, g, b):
    mu = jnp.mean(v, axis=-1, keepdims=True)
    d = v - mu
    var = jnp.mean(d * d, axis=-1, keepdims=True)
    return d * lax.rsqrt(var + NORM_EPS) * g + b


def _mlp_kernel(x_ref, ya_ref, yb_ref, yc_ref, mod_ref, w_out_ref, ln1_ref, w1_ref, w2_ref, ln2_ref,
                o_ref, *, ff_chunk, sub):
    g1, sh2, sc2, g2 = (mod_ref[0, r:r + 1, :] for r in (2, 3, 4, 5))
    tiles = [slice(r0, r0 + sub) for r0 in range(0, x_ref.shape[1], sub)]
    x1s, hs, accs = [], [], []
    for rows in tiles:
        y = (jnp.dot(ya_ref[0, rows, :], w_out_ref[0:256, :], preferred_element_type=F32)
             + jnp.dot(yb_ref[0, rows, :], w_out_ref[256:512, :], preferred_element_type=F32)
             + jnp.dot(yc_ref[0, rows, :], w_out_ref[512:1024, :], preferred_element_type=F32))
        x1s.append(_layer_norm(DEEPNORM_ALPHA * x_ref[0, rows, :] + g1 * y, ln1_ref[0:1, :], ln1_ref[1:2, :]))
        hs.append((x1s[-1] * (1.0 + sc2) + sh2).astype(BF16))
    for h in hs:
        acc = jnp.zeros((sub, D_MODEL), F32)
        for c0 in range(0, D_FF, ff_chunk):
            a = jnp.maximum(jnp.dot(h, w1_ref[:, c0:c0 + ff_chunk], preferred_element_type=F32), 0.0)
            acc = acc + jnp.dot((a * a).astype(BF16), w2_ref[c0:c0 + ff_chunk, :], preferred_element_type=F32)
        accs.append(acc)
    for rows, x1, acc in zip(tiles, x1s, accs):
        o_ref[0, rows, :] = _layer_norm(DEEPNORM_ALPHA * x1 + g2 * acc, ln2_ref[0:1, :], ln2_ref[1:2, :])


def _mlp(x, ya, yb, yc, mod, wl, *, tm):
    bsz, n_tok, _ = x.shape
    tok = lambda width: pl.BlockSpec((1, tm, width), lambda b, i: (b, i, 0))
    const = lambda shape: pl.BlockSpec(shape, lambda b, i: (0, 0), pipeline_mode=pl.Buffered(1))
    return pl.pallas_call(
        functools.partial(_mlp_kernel, ff_chunk=MLP_FF_CHUNK, sub=min(tm, MLP_SUB)),
        grid=(bsz, n_tok // tm),
        in_specs=[tok(D_MODEL), tok(256), tok(256), tok(512),
                  pl.BlockSpec((1, 6, D_MODEL), lambda b, i: (b, 0, 0)),
                  const((D_MODEL, D_MODEL)), const((2, D_MODEL)),
                  const((D_MODEL, D_FF)), const((D_FF, D_MODEL)), const((2, D_MODEL))],
        out_specs=tok(D_MODEL),
        out_shape=jax.ShapeDtypeStruct(x.shape, F32),
        compiler_params=_cparams(("parallel", "parallel")),
        name="outproj_mlp",
    )(x, ya, yb, yc, mod, wl["w_out"], wl["ln1"], wl["w_fc1"], wl["w_fc2"], wl["ln2"])


def _rope_tables(n_tok):
    n_rows = n_tok // GRID_W

    def parts(rot_dim):
        n_freq = rot_dim // 4
        inv = ROPE_THETA ** (-jnp.arange(n_freq, dtype=F32) / n_freq)
        ang_r = jnp.arange(n_rows, dtype=F32)[:, None] * inv
        ang_c = jnp.arange(GRID_W, dtype=F32)[:, None] * inv
        z_r, z_c = jnp.zeros_like(ang_r), jnp.zeros_like(ang_c)
        split = lambda f: (jnp.concatenate([f(ang_r), z_r], axis=-1), jnp.concatenate([z_c, f(ang_c)], axis=-1))
        return split(jnp.cos), split(jnp.sin)

    def expand(r_part, c_part):
        return (r_part[:, None, :] + c_part[None, :, :]).reshape(n_tok, r_part.shape[-1])

    lay_a = lambda v, sign: jnp.tile(jnp.concatenate([sign * v, v], axis=-1), (1, A_HEADS))
    lay_q = lambda v, sign, nope: _mla_lanes(jnp.full(v.shape[:-1] + (C_NOPE_DIM,), nope, F32),
                                             jnp.concatenate([sign * v, v], axis=-1))
    (c64r, c64c), (s64r, s64c) = parts(HEAD_DIM)
    (c32r, c32c), (s32r, s32c) = parts(C_ROPE_DIM)
    cos_a = expand(lay_a(c64r, 1.0), lay_a(c64c, 1.0))
    sin_a = expand(lay_a(s64r, -1.0), lay_a(s64c, -1.0))
    cos_q = expand(lay_q(c32r, 1.0, 0.0), lay_q(c32c, 1.0, 1.0))
    sin_q = expand(lay_q(s32r, -1.0, 0.0), lay_q(s32c, -1.0, 0.0))
    return cos_a, sin_a, cos_q, sin_q


def _mla_lanes(nope, rope):
    half = C_ROPE_DIM // 2
    split = LANES // 2 - half
    pad = jnp.zeros(nope.shape[:-1] + (C_QK_PAD - C_NOPE_DIM - C_ROPE_DIM,), nope.dtype)
    return jnp.concatenate([rope[..., :half], nope[..., :split], rope[..., half:], nope[..., split:], pad], axis=-1)


def _layer_weights(l, w_in, q_norm_b, k_norm_b, mla_q_norm, mla_kv_norm, w_uq, w_uk, w_uv, w_out,
                   ln1_g, ln1_b, w_fc1, w_fc2, ln2_g, ln2_b):
    uq = w_uq[l].reshape(C_Q_RANK, C_HEADS, C_NOPE_DIM + C_ROPE_DIM)
    uq = _mla_lanes(uq[..., :C_NOPE_DIM], uq[..., C_NOPE_DIM:])
    uk = w_uk[l].reshape(C_KV_RANK, C_HEADS, C_NOPE_DIM)
    uk = _mla_lanes(uk, jnp.zeros((C_KV_RANK, C_HEADS, C_ROPE_DIM), F32))
    w_kr = _mla_lanes(jnp.zeros((D_MODEL, C_NOPE_DIM), F32), w_in[l][:, OFF_CKR:OFF_CKR + C_ROPE_DIM])
    uv = w_uv[l].reshape(C_KV_RANK, C_HEADS, C_V_DIM)
    uv = jnp.pad(uv, ((0, 0), (0, 0), (0, V_PAD - C_V_DIM)))
    lane = jnp.arange(256)
    gmat = jnp.where((lane[:, None] // HEAD_DIM) == (lane[None, :] // HEAD_DIM), 1.0 / HEAD_DIM, 0.0)
    return {
        "w_in": w_in[l][:, :OFF_CKR].astype(BF16),
        "w_kr": w_kr.astype(BF16),
        "w_uq": uq.reshape(C_Q_RANK, C_HEADS * C_QK_PAD).astype(BF16),
        "w_uk": uk.reshape(C_KV_RANK, C_HEADS * C_QK_PAD).astype(BF16),
        "w_uv": uv.reshape(C_KV_RANK, C_HEADS * V_PAD).astype(BF16),
        "gq": jnp.tile(q_norm_b[l], B_HEADS)[None, :],
        "gk": jnp.tile(k_norm_b[l], B_KV_HEADS)[None, :],
        "gmat": gmat.astype(BF16),
        "mqn": mla_q_norm[l][None, :],
        "mkvn": mla_kv_norm[l][None, :],
        "w_out": w_out[l].astype(BF16),
        "ln1": jnp.stack([ln1_g[l], ln1_b[l]]),
        "w_fc1": w_fc1[l].astype(BF16),
        "w_fc2": w_fc2[l].astype(BF16),
        "ln2": jnp.stack([ln2_g[l], ln2_b[l]]),
    }


def kernel(x, c, ctx, c_ctx, w_mod, b_mod, w_in, sink_a, q_norm_b, k_norm_b, mla_q_norm, mla_kv_norm,
           w_uq, w_uk, w_uv, w_out, ln1_g, ln1_b, w_fc1, w_fc2, ln2_g, ln2_b):
    bsz, n_tok, _ = x.shape
    n_ctx = ctx.shape[1]
    cc = jnp.concatenate([c, c_ctx[None, :], jnp.zeros((8 - bsz - 1, D_MODEL), F32)], axis=0)
    mods = _modulation(cc, w_mod, b_mod)
    tabs = _rope_tables(n_tok)
    xc = ctx
    for l in range(DEPTH):
        last = l == DEPTH - 1
        wl = _layer_weights(l, w_in, q_norm_b, k_norm_b, mla_q_norm, mla_kv_norm, w_uq, w_uk, w_uv, w_out,
                            ln1_g, ln1_b, w_fc1, w_fc2, ln2_g, ln2_b)
        mod_lat = mods[l, 0:bsz].reshape(bsz, 6, D_MODEL)
        mod_ctx = mods[l, bsz].reshape(1, 6, D_MODEL)
        qkv = _inproj(x, xc, mod_lat, mod_ctx, wl, tabs, tm=INPROJ_TM)
        qa, ka, va, qb, kb, vb, qc, kc, vc = qkv
        rows = dict(n_tok=n_tok, n_ctx=n_ctx)
        keys = dict(n_tok=n_tok, n_keys=n_tok + n_ctx, tq=FLASH_TQ, tk=FLASH_TK)
        ya = _window_attn(sink_a[l], qa, ka, va, tq=WINDOW_TQ, n_sub=WINDOW_TILES_PER_STEP, **rows)
        yb = _flash(qb, kb, vb, shared_kv=True, **keys)
        yc = _flash(qc, kc, vc, shared_kv=False, **keys)
        x = _mlp(x, ya, yb, yc, mod_lat, wl, tm=MLP_TM)
        if not last:
            yca, ycb, ycc = _ctx_attn(sink_a[l], qkv, **rows)
            xc = _mlp(xc, yca, ycb, ycc, jnp.broadcast_to(mod_ctx, (bsz, 6, D_MODEL)), wl, tm=n_ctx)
    return x
```

```python
import functools

import jax
import jax.numpy as jnp
from jax import lax
from jax.experimental import pallas as pl
from jax.experimental.pallas import tpu as pltpu

F32 = jnp.float32
BF16 = jnp.bfloat16

D_MODEL = 1024
DEPTH = 2
GRID_W = 64
HEAD_DIM = 64
WINDOW = 128
A_HEADS, A_KV_HEADS = 4, 2
B_HEADS, B_KV_HEADS = 4, 2
C_HEADS = 8
C_Q_RANK, C_KV_RANK = 256, 128
C_NOPE_DIM, C_ROPE_DIM, C_V_DIM = 64, 32, 64
D_FF = 4 * D_MODEL
ROPE_THETA = 10000.0
NORM_EPS = 1e-6
NEG_INF = -1e30
DEEPNORM_ALPHA = (2 * DEPTH) ** 0.25

OFF_AQ, OFF_AK, OFF_AV = 0, 256, 384
OFF_BQ, OFF_BK, OFF_BV = 512, 768, 896
OFF_CQ, OFF_CKV, OFF_CKR = 1024, 1280, 1408
LANES = 128
C_QK_PAD = 128
V_PAD = 128
LOG2E = 1.4426950408889634

VMEM_LIMIT = 56 * 1024 * 1024

INPROJ_TM = 512
FLASH_TQ, FLASH_TK = 1024, 2816
WINDOW_TQ, WINDOW_TILES_PER_STEP = 256, 8
MLP_TM, MLP_SUB, MLP_FF_CHUNK = 512, 256, 1024
MOD_TN = 1024


def _cparams(sem):
    return pltpu.CompilerParams(dimension_semantics=sem, vmem_limit_bytes=VMEM_LIMIT)


def _full(shape):
    n = len(shape)
    return pl.BlockSpec(shape, lambda *_: (0,) * n)


def _mod_kernel(cc_ref, w_ref, b_ref, o_ref):
    cc = cc_ref[...]
    s = cc / (1.0 + jnp.exp(-cc))
    w = w_ref[0]
    s_hi, w_hi = s.astype(BF16), w.astype(BF16)
    s_lo = (s - s_hi.astype(F32)).astype(BF16)
    w_lo = (w - w_hi.astype(F32)).astype(BF16)
    o_ref[0] = (jnp.dot(s_hi, w_hi, preferred_element_type=F32) + jnp.dot(s_lo, w_hi, preferred_element_type=F32)
                + jnp.dot(s_hi, w_lo, preferred_element_type=F32) + b_ref[0])


def _modulation(cc, w_mod, b_mod):
    n_l = w_mod.shape[0]
    tn = MOD_TN
    return pl.pallas_call(
        _mod_kernel,
        grid=(n_l, 6 * D_MODEL // tn),
        in_specs=[pl.BlockSpec((8, D_MODEL), lambda l, j: (0, 0)),
                  pl.BlockSpec((1, D_MODEL, tn), lambda l, j: (l, 0, j)),
                  pl.BlockSpec((1, 1, tn), lambda l, j: (l, 0, j))],
        out_specs=pl.BlockSpec((1, 8, tn), lambda l, j: (l, 0, j)),
        out_shape=jax.ShapeDtypeStruct((n_l, 8, 6 * D_MODEL), F32),
        compiler_params=_cparams(("parallel", "parallel")),
        name="modulation",
    )(cc, w_mod, b_mod.reshape(n_l, 1, 6 * D_MODEL))


def _rope(x, cos, sin_signed, half):
    width = x.shape[-1]
    lane = lax.broadcasted_iota(jnp.int32, x.shape, 1)
    first = (lane % (2 * half)) < half
    nxt = pltpu.roll(x, width - half, axis=1)
    prv = pltpu.roll(x, half, axis=1)
    return x * cos + jnp.where(first, nxt, prv) * sin_signed


def _rope_halves(x, cos, sin_signed):
    return x * cos + pltpu.roll(x, LANES // 2, axis=1) * sin_signed


def _group_mean_sq(x, g_ref):
    x2 = x * x
    hi = x2.astype(BF16)
    lo = (x2 - hi.astype(F32)).astype(BF16)
    g = g_ref[...]
    return (jnp.dot(hi, g, preferred_element_type=F32) + jnp.dot(lo, g, preferred_element_type=F32))


def _row_rms(x, gain):
    ms = jnp.mean(x * x, axis=-1, keepdims=True)
    return x * lax.rsqrt(ms + NORM_EPS) * gain


def _inproj_kernel(x_ref, xc_ref, modl_ref, modc_ref, w_in_ref, w_kr_ref, w_uq_ref, w_uk_ref, w_uv_ref,
                   gq_ref, gk_ref, gmat_ref, mqn_ref, mkvn_ref, cos_a_ref, sin_a_ref, cos_q_ref, sin_q_ref,
                   qa_ref, ka_ref, va_ref, qb_ref, kb_ref, vb_ref, qc_ref, kc_ref, vc_ref):
    is_ctx = pl.program_id(0) == pl.num_programs(0) - 1
    x = jnp.where(is_ctx, xc_ref[0], x_ref[0])
    shift = jnp.where(is_ctx, modc_ref[0, 0:1, :], modl_ref[0, 0:1, :])
    scale = jnp.where(is_ctx, modc_ref[0, 1:2, :], modl_ref[0, 1:2, :])
    cos_a, sin_a = jnp.where(is_ctx, 1.0, cos_a_ref[...]), jnp.where(is_ctx, 0.0, sin_a_ref[...])
    cos_q, sin_q = jnp.where(is_ctx, 1.0, cos_q_ref[...]), jnp.where(is_ctx, 0.0, sin_q_ref[...])
    h = (x * (1.0 + scale) + shift).astype(BF16)
    p = jnp.dot(h, w_in_ref[...], preferred_element_type=F32)

    def store_heads(ref, val, n_heads, width):
        for hh in range(n_heads):
            ref[0, hh] = val[:, hh * width:(hh + 1) * width].astype(ref.dtype)

    v_lanes = lax.broadcasted_iota(jnp.int32, (x.shape[0], V_PAD), 1) < HEAD_DIM

    def store_values(ref, val):
        ref[0, 0] = jnp.where(v_lanes, val, 1.0).astype(BF16)
        ref[0, 1] = jnp.where(v_lanes, pltpu.roll(val, HEAD_DIM, axis=1), 1.0).astype(BF16)

    qa = p[:, OFF_AQ:OFF_AQ + 256]
    ka = p[:, OFF_AK:OFF_AK + 128]
    qa = _rope(qa, cos_a, sin_a, HEAD_DIM // 2)
    ka = _rope(ka, cos_a[:, 0:128], sin_a[:, 0:128], HEAD_DIM // 2)
    store_heads(qa_ref, qa * (HEAD_DIM ** -0.5 * LOG2E), A_HEADS, HEAD_DIM)
    store_heads(ka_ref, ka, A_KV_HEADS, HEAD_DIM)
    store_values(va_ref, p[:, OFF_AV:OFF_AV + 128])

    qb = p[:, OFF_BQ:OFF_BQ + 256]
    kb = p[:, OFF_BK:OFF_BK + 128]
    qb = qb * lax.rsqrt(_group_mean_sq(qb, gmat_ref) + NORM_EPS) * gq_ref[...]
    kb = kb * lax.rsqrt(_group_mean_sq(kb, gmat_ref.at[0:128, 0:128]) + NORM_EPS) * gk_ref[...]
    qb = _rope(qb, cos_a, sin_a, HEAD_DIM // 2)
    kb = _rope(kb, cos_a[:, 0:128], sin_a[:, 0:128], HEAD_DIM // 2)
    store_heads(qb_ref, qb * (HEAD_DIM ** -0.5 * LOG2E), B_HEADS, HEAD_DIM)
    store_heads(kb_ref, kb, B_KV_HEADS, HEAD_DIM)
    store_values(vb_ref, p[:, OFF_BV:OFF_BV + 128])

    cq = _row_rms(p[:, OFF_CQ:OFF_CQ + C_Q_RANK], mqn_ref[...]).astype(BF16)
    ckv = _row_rms(p[:, OFF_CKV:OFF_CKV + C_KV_RANK], mkvn_ref[...]).astype(BF16)
    q = jnp.dot(cq, w_uq_ref[...], preferred_element_type=F32)
    kn = jnp.dot(ckv, w_uk_ref[...], preferred_element_type=F32)
    vc = jnp.dot(ckv, w_uv_ref[...], preferred_element_type=F32)
    kr = jnp.dot(h, w_kr_ref[...], preferred_element_type=F32)
    kr = _rope_halves(kr, cos_q, sin_q)
    c_scale = (C_NOPE_DIM + C_ROPE_DIM) ** -0.5 * LOG2E
    for hh in range(C_HEADS):
        qh = _rope_halves(q[:, hh * C_QK_PAD:(hh + 1) * C_QK_PAD], cos_q, sin_q)
        qc_ref[0, hh] = (qh * c_scale).astype(BF16)
        kc_ref[0, hh] = (kn[:, hh * C_QK_PAD:(hh + 1) * C_QK_PAD] + kr).astype(BF16)
        vc_ref[0, hh] = jnp.where(v_lanes, vc[:, hh * V_PAD:(hh + 1) * V_PAD], 1.0).astype(BF16)


def _inproj(x, xc, mod_lat, mod_ctx, wl, tabs, *, tm):
    bsz, n_tok, _ = x.shape
    n_ctx = xc.shape[1]
    assert n_tok % tm == 0 and n_ctx <= tm
    nt = n_tok // tm
    xc_pad = jnp.pad(xc, ((0, 0), (0, tm - n_ctx), (0, 0)))
    out_dims = ((A_HEADS, HEAD_DIM), (A_KV_HEADS, HEAD_DIM), (A_KV_HEADS, V_PAD),
                (B_HEADS, HEAD_DIM), (B_KV_HEADS, HEAD_DIM), (B_KV_HEADS, V_PAD),
                (C_HEADS, C_QK_PAD), (C_HEADS, C_QK_PAD), (C_HEADS, V_PAD))
    lat = lambda i: jnp.minimum(i, nt - 1)
    tok = lambda width: pl.BlockSpec((tm, width), lambda i, b: (lat(i), 0))
    return pl.pallas_call(
        _inproj_kernel,
        grid=(nt + 1, bsz),
        in_specs=[pl.BlockSpec((1, tm, D_MODEL), lambda i, b: (b, lat(i), 0)),
                  pl.BlockSpec((1, tm, D_MODEL), lambda i, b: (jnp.where(i == nt, b, 0), 0, 0)),
                  pl.BlockSpec((1, 6, D_MODEL), lambda i, b: (b, 0, 0)), _full((1, 6, D_MODEL)),
                  _full((D_MODEL, OFF_CKR)), _full((D_MODEL, C_QK_PAD)), _full((C_Q_RANK, C_HEADS * C_QK_PAD)),
                  _full((C_KV_RANK, C_HEADS * C_QK_PAD)), _full((C_KV_RANK, C_HEADS * V_PAD)),
                  _full((1, 256)), _full((1, 128)), _full((256, 256)),
                  _full((1, C_Q_RANK)), _full((1, C_KV_RANK)),
                  tok(256), tok(256), tok(128), tok(128)],
        out_specs=[pl.BlockSpec((1, n, tm, width), lambda i, b: (b, 0, i, 0)) for n, width in out_dims],
        out_shape=[jax.ShapeDtypeStruct((bsz, n, (nt + 1) * tm, width), BF16) for n, width in out_dims],
        compiler_params=_cparams(("arbitrary", "parallel")),
        name="inproj",
    )(x, xc_pad, mod_lat, mod_ctx, wl["w_in"], wl["w_kr"], wl["w_uq"], wl["w_uk"], wl["w_uv"], wl["gq"], wl["gk"],
      wl["gmat"], wl["mqn"], wl["mkvn"], *tabs)


def _qk(q, k):
    return lax.dot_general(q, k, (((1,), (1,)), ((), ())), preferred_element_type=F32)


def _online_step(q, k, v, m, acc):
    s = _qk(q, k)
    m_new = jnp.maximum(m, jnp.max(s, axis=-1, keepdims=True))
    p = jnp.exp2(s - m_new)
    acc = jnp.exp2(m - m_new) * acc + jnp.dot(p.astype(v.dtype), v, preferred_element_type=F32)
    return m_new, acc


def _softmax_attend(q, k, v, sink=None):
    s = _qk(q, k)
    m = jnp.max(s, axis=-1, keepdims=True)
    if sink is not None:
        m = jnp.maximum(m, sink)
    p = jnp.exp2(s - m)
    l = jnp.sum(p, axis=-1, keepdims=True)
    if sink is not None:
        l = l + jnp.exp2(sink - m)
    return jnp.dot(p.astype(v.dtype), v, preferred_element_type=F32) / l


def _flash_kernel(q_ref, k_ref, v_ref, o_ref, *, shared_kv, tk):
    tq = q_ref.shape[2]
    dv = o_ref.shape[2] // 2
    heads = ((q_ref[0, 0], 0), (q_ref[0, 1], 0 if shared_kv else 1))
    carry = [(jnp.full((tq, 1), NEG_INF, F32), jnp.zeros((tq, v_ref.shape[3]), F32)) for _ in heads]
    for off in range(0, k_ref.shape[2], tk):
        carry = [_online_step(q, k_ref[0, hk, off:off + tk, :], v_ref[0, hk, off:off + tk, :], *c)
                 for (q, hk), c in zip(heads, carry)]
    outs = [(acc / pltpu.roll(acc, dv, axis=1))[:, :dv] for _, acc in carry]
    o_ref[0] = jnp.concatenate(outs, axis=-1).astype(o_ref.dtype)


def _flash(q, k, v, *, n_tok, n_keys, shared_kv, tq, tk):
    bsz, n_heads, _, dk = q.shape
    dvp = v.shape[3]
    dv = dvp // 2
    assert n_tok % tq == 0 and n_keys % tk == 0
    kvb = 1 if shared_kv else 2
    kv_spec = lambda width: pl.BlockSpec((1, kvb, n_keys, width), lambda b, hp, i: (b, hp, 0, 0))
    return pl.pallas_call(
        functools.partial(_flash_kernel, shared_kv=shared_kv, tk=tk),
        grid=(bsz, n_heads // 2, n_tok // tq),
        in_specs=[pl.BlockSpec((1, 2, tq, dk), lambda b, hp, i: (b, hp, i, 0)), kv_spec(dk), kv_spec(dvp)],
        out_specs=pl.BlockSpec((1, tq, 2 * dv), lambda b, hp, i: (b, i, hp)),
        out_shape=jax.ShapeDtypeStruct((bsz, n_tok, n_heads * dv), BF16),
        compiler_params=_cparams(("parallel", "parallel", "arbitrary")),
        name="flash_gqa" if shared_kv else "flash_mla",
    )(q, k, v)


def _window_kernel(sink_ref, q_ref, bias_ref, kc_ref, vc_ref, kl_ref, vl_ref, o_ref):
    tq, span = bias_ref.shape[1], bias_ref.shape[2]
    d, n_tok = q_ref.shape[3], kl_ref.shape[2]
    n_sub = q_ref.shape[2] // tq
    n_tiles = n_tok // tq
    row1 = lax.broadcasted_iota(jnp.int32, (2 * tq, 1), 0)
    for sub in range(n_sub):
        tile = pl.program_id(1) * n_sub + sub
        start = pl.multiple_of(jnp.clip(tile * tq - WINDOW, 0, n_tok - span), WINDOW)
        bias = bias_ref[jnp.where(tile == 0, 0, jnp.where(tile == n_tiles - 1, 2, 1))]
        rows = slice(sub * tq, (sub + 1) * tq)
        outs = []
        for kvh in range(kl_ref.shape[1]):
            q = q_ref[0, 2 * kvh:2 * kvh + 2, rows, :].reshape(2 * tq, d)
            s_loc = (_qk(q, kl_ref[0, kvh, pl.ds(start, span), :]).reshape(2, tq, span) + bias).reshape(2 * tq, span)
            s_ctx = _qk(q, kc_ref[0, kvh])
            sink = jnp.where(row1 < tq, sink_ref[2 * kvh], sink_ref[2 * kvh + 1]) * LOG2E
            m = jnp.maximum(jnp.maximum(jnp.max(s_loc, axis=-1, keepdims=True),
                                        jnp.max(s_ctx, axis=-1, keepdims=True)), sink)
            acc = (jnp.dot(jnp.exp2(s_loc - m).astype(BF16), vl_ref[0, kvh, pl.ds(start, span), :],
                           preferred_element_type=F32)
                   + jnp.dot(jnp.exp2(s_ctx - m).astype(BF16), vc_ref[0, kvh], preferred_element_type=F32))
            o = (acc / (pltpu.roll(acc, d, axis=1) + jnp.exp2(sink - m)))[:, :d]
            outs += [o[:tq], o[tq:]]
        o_ref[0, rows, :] = jnp.concatenate(outs, axis=-1).astype(o_ref.dtype)


def _window_attn(sink, q, k, v, *, n_tok, n_ctx, tq, n_sub):
    bsz, n_heads, _, d = q.shape
    tqs = tq * n_sub
    span = tq + 2 * WINDOW
    assert n_tok % tqs == 0 and n_tok // tq >= 2 and span <= n_tok and tq % WINDOW == 0 and n_tok % n_ctx == 0
    rel = jnp.arange(span)[None, None, :] - jnp.arange(tq)[None, :, None] + jnp.array([0, -WINDOW, tq - span])[:, None, None]
    bias = jnp.where(jnp.abs(rel) <= WINDOW, 0.0, NEG_INF).astype(F32)
    ctx_spec = lambda a: pl.BlockSpec((1, a.shape[1], n_ctx, a.shape[3]), lambda b, i: (b, 0, n_tok // n_ctx, 0))
    lat_spec = lambda a: pl.BlockSpec((1, a.shape[1], n_tok, a.shape[3]), lambda b, i: (b, 0, 0, 0))
    return pl.pallas_call(
        _window_kernel,
        grid=(bsz, n_tok // tqs),
        in_specs=[pl.BlockSpec(memory_space=pltpu.SMEM),
                  pl.BlockSpec((1, n_heads, tqs, d), lambda b, i: (b, 0, i, 0)),
                  pl.BlockSpec((3, tq, span), lambda b, i: (0, 0, 0)),
                  ctx_spec(k), ctx_spec(v), lat_spec(k), lat_spec(v)],
        out_specs=pl.BlockSpec((1, tqs, n_heads * d), lambda b, i: (b, i, 0)),
        out_shape=jax.ShapeDtypeStruct((bsz, n_tok, n_heads * d), BF16),
        compiler_params=_cparams(("parallel", "arbitrary")),
        name="window_gqa",
    )(sink, q, bias, k, v, k, v)


def _ctx_attn_kernel(sink_ref, qa_ref, ka_ref, va_ref, qb_ref, kb_ref, vb_ref, qc_ref, kc_ref, vc_ref,
                     ya_ref, yb_ref, yc_ref):
    n = qa_ref.shape[2]
    row1 = lax.broadcasted_iota(jnp.int32, (2 * n, 1), 0)

    def gqa(q_ref, k_ref, v_ref, y_ref, with_sink):
        outs = []
        for kvh in range(k_ref.shape[1]):
            q = q_ref[0, 2 * kvh:2 * kvh + 2].reshape(2 * n, q_ref.shape[3])
            sink = (jnp.where(row1 < n, sink_ref[2 * kvh], sink_ref[2 * kvh + 1]) * LOG2E
                    if with_sink else None)
            o = _softmax_attend(q, k_ref[0, kvh], v_ref[0, kvh, :, 0:HEAD_DIM], sink)
            outs += [o[:n], o[n:]]
        y_ref[0] = jnp.concatenate(outs, axis=-1).astype(y_ref.dtype)

    gqa(qa_ref, ka_ref, va_ref, ya_ref, True)
    gqa(qb_ref, kb_ref, vb_ref, yb_ref, False)
    outs = [_softmax_attend(qc_ref[0, hh], kc_ref[0, hh], vc_ref[0, hh, :, 0:C_V_DIM]) for hh in range(C_HEADS)]
    yc_ref[0] = jnp.concatenate(outs, axis=-1).astype(yc_ref.dtype)


def _ctx_attn(sink, cx, *, n_tok, n_ctx):
    bsz, n = cx[0].shape[0], n_ctx
    assert n_tok % n_ctx == 0
    spec = lambda a: pl.BlockSpec((1, a.shape[1], n, a.shape[3]), lambda b: (b, 0, n_tok // n, 0))
    widths = (A_HEADS * HEAD_DIM, B_HEADS * HEAD_DIM, C_HEADS * C_V_DIM)
    return pl.pallas_call(
        _ctx_attn_kernel,
        grid=(bsz,),
        in_specs=[pl.BlockSpec(memory_space=pltpu.SMEM)] + [spec(a) for a in cx],
        out_specs=[pl.BlockSpec((1, n, w), lambda b: (b, 0, 0)) for w in widths],
        out_shape=[jax.ShapeDtypeStruct((bsz, n, w), BF16) for w in widths],
        compiler_params=_cparams(("parallel",)),
        name="ctx_attn",
    )(sink, *cx)


def _layer_norm(v, g, b):
    mu = jnp.mean(v, axis=-1, keepdims=True)
    d = v - mu
    var = jnp.mean(d * d, axis=-1, keepdims=True)
    return d * lax.rsqrt(var + NORM_EPS) * g + b


def _mlp_kernel(x_ref, ya_ref, yb_ref, yc_ref, mod_ref, w_out_ref, ln1_ref, w1_ref, w2_ref, ln2_ref,
                o_ref, *, ff_chunk, sub):
    g1, sh2, sc2, g2 = (mod_ref[0, r:r + 1, :] for r in (2, 3, 4, 5))
    tiles = [slice(r0, r0 + sub) for r0 in range(0, x_ref.shape[1], sub)]
    x1s, hs, accs = [], [], []
    for rows in tiles:
        y = (jnp.dot(ya_ref[0, rows, :], w_out_ref[0:256, :], preferred_element_type=F32)
             + jnp.dot(yb_ref[0, rows, :], w_out_ref[256:512, :], preferred_element_type=F32)
             + jnp.dot(yc_ref[0, rows, :], w_out_ref[512:1024, :], preferred_element_type=F32))
        x1s.append(_layer_norm(DEEPNORM_ALPHA * x_ref[0, rows, :] + g1 * y, ln1_ref[0:1, :], ln1_ref[1:2, :]))
        hs.append((x1s[-1] * (1.0 + sc2) + sh2).astype(BF16))
    for h in hs:
        acc = jnp.zeros((sub, D_MODEL), F32)
        for c0 in range(0, D_FF, ff_chunk):
            a = jnp.maximum(jnp.dot(h, w1_ref[:, c0:c0 + ff_chunk], preferred_element_type=F32), 0.0)
            acc = acc + jnp.dot((a * a).astype(BF16), w2_ref[c0:c0 + ff_chunk, :], preferred_element_type=F32)
        accs.append(acc)
    for rows, x1, acc in zip(tiles, x1s, accs):
        o_ref[0, rows, :] = _layer_norm(DEEPNORM_ALPHA * x1 + g2 * acc, ln2_ref[0:1, :], ln2_ref[1:2, :])


def _mlp(x, ya, yb, yc, mod, wl, *, tm):
    bsz, n_tok, _ = x.shape
    tok = lambda width: pl.BlockSpec((1, tm, width), lambda b, i: (b, i, 0))
    const = lambda shape: pl.BlockSpec(shape, lambda b, i: (0, 0), pipeline_mode=pl.Buffered(1))
    return pl.pallas_call(
        functools.partial(_mlp_kernel, ff_chunk=MLP_FF_CHUNK, sub=min(tm, MLP_SUB)),
        grid=(bsz, n_tok // tm),
        in_specs=[tok(D_MODEL), tok(256), tok(256), tok(512),
                  pl.BlockSpec((1, 6, D_MODEL), lambda b, i: (b, 0, 0)),
                  const((D_MODEL, D_MODEL)), const((2, D_MODEL)),
                  const((D_MODEL, D_FF)), const((D_FF, D_MODEL)), const((2, D_MODEL))],
        out_specs=tok(D_MODEL),
        out_shape=jax.ShapeDtypeStruct(x.shape, F32),
        compiler_params=_cparams(("parallel", "parallel")),
        name="outproj_mlp",
    )(x, ya, yb, yc, mod, wl["w_out"], wl["ln1"], wl["w_fc1"], wl["w_fc2"], wl["ln2"])


def _rope_tables(n_tok):
    n_rows = n_tok // GRID_W

    def parts(rot_dim):
        n_freq = rot_dim // 4
        inv = ROPE_THETA ** (-jnp.arange(n_freq, dtype=F32) / n_freq)
        ang_r = jnp.arange(n_rows, dtype=F32)[:, None] * inv
        ang_c = jnp.arange(GRID_W, dtype=F32)[:, None] * inv
        z_r, z_c = jnp.zeros_like(ang_r), jnp.zeros_like(ang_c)
        split = lambda f: (jnp.concatenate([f(ang_r), z_r], axis=-1), jnp.concatenate([z_c, f(ang_c)], axis=-1))
        return split(jnp.cos), split(jnp.sin)

    def expand(r_part, c_part):
        return (r_part[:, None, :] + c_part[None, :, :]).reshape(n_tok, r_part.shape[-1])

    lay_a = lambda v, sign: jnp.tile(jnp.concatenate([sign * v, v], axis=-1), (1, A_HEADS))
    lay_q = lambda v, sign, nope: _mla_lanes(jnp.full(v.shape[:-1] + (C_NOPE_DIM,), nope, F32),
                                             jnp.concatenate([sign * v, v], axis=-1))
    (c64r, c64c), (s64r, s64c) = parts(HEAD_DIM)
    (c32r, c32c), (s32r, s32c) = parts(C_ROPE_DIM)
    cos_a = expand(lay_a(c64r, 1.0), lay_a(c64c, 1.0))
    sin_a = expand(lay_a(s64r, -1.0), lay_a(s64c, -1.0))
    cos_q = expand(lay_q(c32r, 1.0, 0.0), lay_q(c32c, 1.0, 1.0))
    sin_q = expand(lay_q(s32r, -1.0, 0.0), lay_q(s32c, -1.0, 0.0))
    return cos_a, sin_a, cos_q, sin_q


def _mla_lanes(nope, rope):
    half = C_ROPE_DIM // 2
    split = LANES // 2 - half
    pad = jnp.zeros(nope.shape[:-1] + (C_QK_PAD - C_NOPE_DIM - C_ROPE_DIM,), nope.dtype)
    return jnp.concatenate([rope[..., :half], nope[..., :split], rope[..., half:], nope[..., split:], pad], axis=-1)


def _layer_weights(l, w_in, q_norm_b, k_norm_b, mla_q_norm, mla_kv_norm, w_uq, w_uk, w_uv, w_out,
                   ln1_g, ln1_b, w_fc1, w_fc2, ln2_g, ln2_b):
    uq = w_uq[l].reshape(C_Q_RANK, C_HEADS, C_NOPE_DIM + C_ROPE_DIM)
    uq = _mla_lanes(uq[..., :C_NOPE_DIM], uq[..., C_NOPE_DIM:])
    uk = w_uk[l].reshape(C_KV_RANK, C_HEADS, C_NOPE_DIM)
    uk = _mla_lanes(uk, jnp.zeros((C_KV_RANK, C_HEADS, C_ROPE_DIM), F32))
    w_kr = _mla_lanes(jnp.zeros((D_MODEL, C_NOPE_DIM), F32), w_in[l][:, OFF_CKR:OFF_CKR + C_ROPE_DIM])
    uv = w_uv[l].reshape(C_KV_RANK, C_HEADS, C_V_DIM)
    uv = jnp.pad(uv, ((0, 0), (0, 0), (0, V_PAD - C_V_DIM)))
    lane = jnp.arange(256)
    gmat = jnp.where((lane[:, None] // HEAD_DIM) == (lane[None, :] // HEAD_DIM), 1.0 / HEAD_DIM, 0.0)
    return {
        "w_in": w_in[l][:, :OFF_CKR].astype(BF16),
        "w_kr": w_kr.astype(BF16),
        "w_uq": uq.reshape(C_Q_RANK, C_HEADS * C_QK_PAD).astype(BF16),
        "w_uk": uk.reshape(C_KV_RANK, C_HEADS * C_QK_PAD).astype(BF16),
        "w_uv": uv.reshape(C_KV_RANK, C_HEADS * V_PAD).astype(BF16),
        "gq": jnp.tile(q_norm_b[l], B_HEADS)[None, :],
        "gk": jnp.tile(k_norm_b[l], B_KV_HEADS)[None, :],
        "gmat": gmat.astype(BF16),
        "mqn": mla_q_norm[l][None, :],
        "mkvn": mla_kv_norm[l][None, :],
        "w_out": w_out[l].astype(BF16),
        "ln1": jnp.stack([ln1_g[l], ln1_b[l]]),
        "w_fc1": w_fc1[l].astype(BF16),
        "w_fc2": w_fc2[l].astype(BF16),
        "ln2": jnp.stack([ln2_g[l], ln2_b[l]]),
    }


def kernel(x, c, ctx, c_ctx, w_mod, b_mod, w_in, sink_a, q_norm_b, k_norm_b, mla_q_norm, mla_kv_norm,
           w_uq, w_uk, w_uv, w_out, ln1_g, ln1_b, w_fc1, w_fc2, ln2_g, ln2_b):
    bsz, n_tok, _ = x.shape
    n_ctx = ctx.shape[1]
    cc = jnp.concatenate([c, c_ctx[None, :], jnp.zeros((8 - bsz - 1, D_MODEL), F32)], axis=0)
    mods = _modulation(cc, w_mod, b_mod)
    tabs = _rope_tables(n_tok)
    xc = ctx
    for l in range(DEPTH):
        last = l == DEPTH - 1
        wl = _layer_weights(l, w_in, q_norm_b, k_norm_b, mla_q_norm, mla_kv_norm, w_uq, w_uk, w_uv, w_out,
                            ln1_g, ln1_b, w_fc1, w_fc2, ln2_g, ln2_b)
        mod_lat = mods[l, 0:bsz].reshape(bsz, 6, D_MODEL)
        mod_ctx = mods[l, bsz].reshape(1, 6, D_MODEL)
        qkv = _inproj(x, xc, mod_lat, mod_ctx, wl, tabs, tm=INPROJ_TM)
        qa, ka, va, qb, kb, vb, qc, kc, vc = qkv
        rows = dict(n_tok=n_tok, n_ctx=n_ctx)
        keys = dict(n_tok=n_tok, n_keys=n_tok + n_ctx, tq=FLASH_TQ, tk=FLASH_TK)
        ya = _window_attn(sink_a[l], qa, ka, va, tq=WINDOW_TQ, n_sub=WINDOW_TILES_PER_STEP, **rows)
        yb = _flash(qb, kb, vb, shared_kv=True, **keys)
        yc = _flash(qc, kc, vc, shared_kv=False, **keys)
        x = _mlp(x, ya, yb, yc, mod_lat, wl, tm=MLP_TM)
        if not last:
            yca, ycb, ycc = _ctx_attn(sink_a[l], qkv, **rows)
            xc = _mlp(xc, yca, ycb, ycc, jnp.broadcast_to(mod_ctx, (bsz, 6, D_MODEL)), wl, tm=n_ctx)
    return x
```
